```python
import math
import jax
import jax.numpy as jnp
from jax import lax
import numpy as np

D_MODEL = 4096
BATCH = 4
SEQ = 2048
DEPTH = 2
DEC_BATCH = 8
DEC_SEQ = 1
PAST_LEN = 16384
PAGE_SIZE = 128

HEAD_DIM = 128
ROPE_THETA = 10000.0
EPS = 1e-6
NEG = -1e30
TINY = 1e-30

DIFF_HEADS = 8
DIFF_KV_HEADS = 4
DIFF_DH = HEAD_DIM // 2
DIFF_DV = HEAD_DIM
DIFF_Q_BLOCK = 128

MOBA_HEADS = 8
MOBA_KV_HEADS = 4
MOBA_BLOCK = 256
MOBA_TOPK = 3
MOBA_Q_BLOCK = 32

NSA_HEADS = 16
NSA_KV_HEADS = 2
CMP_BLOCK = 32
CMP_STRIDE = 16
CMP_HIDDEN = 256
SEL_BLOCK = 64
SEL_TOPK = 16
WINDOW = 512
NSA_Q_BLOCK = 64
FORCE_SCORE = 1e4

DIFF_WIDTH = DIFF_HEADS * DIFF_DV
MOBA_WIDTH = MOBA_HEADS * HEAD_DIM
NSA_WIDTH = NSA_HEADS * HEAD_DIM
MIX_WIDTH = DIFF_WIDTH + MOBA_WIDTH + NSA_WIDTH
N_BRANCH = 3
D_FF = 11008

DIFF_Q_COLS = DIFF_HEADS * 2 * DIFF_DH
DIFF_K_COLS = DIFF_KV_HEADS * 2 * DIFF_DH
DIFF_V_COLS = DIFF_KV_HEADS * DIFF_DV
MOBA_Q_COLS = MOBA_HEADS * HEAD_DIM
MOBA_KV_COLS = MOBA_KV_HEADS * HEAD_DIM
NSA_Q_COLS = NSA_HEADS * HEAD_DIM
NSA_KV_COLS = NSA_KV_HEADS * HEAD_DIM
NSA_GATE_COLS = NSA_HEADS * 3
MERGE_GATE_COLS = N_BRANCH * D_MODEL
IN_COLS = (DIFF_Q_COLS + DIFF_K_COLS + DIFF_V_COLS + MOBA_Q_COLS + 2 * MOBA_KV_COLS
           + NSA_Q_COLS + 6 * NSA_KV_COLS + NSA_GATE_COLS + MERGE_GATE_COLS)

kernel_name = 'hybrid_diff_moba_nsa_macaron_step'


def rmsnorm(x, g):
    xf = x.astype(jnp.float32)
    y = xf * lax.rsqrt(jnp.mean(xf * xf, axis=-1, keepdims=True) + EPS)
    return (y * g.astype(jnp.float32)).astype(x.dtype)


def rope(x, pos):
    d = x.shape[-1]
    inv = ROPE_THETA ** (-jnp.arange(0, d, 2, dtype=jnp.float32) / d)
    ang = pos.astype(jnp.float32)[:, None] * inv[None, :]
    shape = (pos.shape[0],) + (1,) * (x.ndim - 3) + (d // 2,)
    cos = jnp.cos(ang).reshape(shape)
    sin = jnp.sin(ang).reshape(shape)
    x1, x2 = jnp.split(x.astype(jnp.float32), 2, axis=-1)
    return jnp.concatenate([x1 * cos - x2 * sin, x2 * cos + x1 * sin], axis=-1).astype(x.dtype)


def masked_softmax(s, mask):
    s = jnp.where(mask, s, NEG)
    p = jnp.exp(s - jnp.max(s, axis=-1, keepdims=True)) * mask
    return p / jnp.maximum(jnp.sum(p, axis=-1, keepdims=True), TINY)


def swiglu_half(x, g, w_gu, w_down):
    gate, up = jnp.split(rmsnorm(x, g) @ w_gu, 2, axis=-1)
    return (jax.nn.silu(gate) * up) @ w_down


def sweep(fn, T, block):
    out = lax.map(fn, jnp.arange(T // block))
    return jnp.moveaxis(out, 0, 1).reshape(out.shape[1], T, out.shape[-1])


def project(h, pos, w_in, diff_qk_norm, moba_qk_norm, nsa_qk_norm):
    B, T, _ = h.shape
    sizes = (DIFF_Q_COLS, DIFF_K_COLS, DIFF_V_COLS, MOBA_Q_COLS, MOBA_KV_COLS, MOBA_KV_COLS,
             NSA_Q_COLS) + (NSA_KV_COLS,) * 6 + (NSA_GATE_COLS, MERGE_GATE_COLS)
    idx = []
    acc = 0
    for s in sizes[:-1]:
        acc += s
        idx.append(acc)
    (dq, dk, dv, mq, mk, mv, nq, nck, ncv, nsk, nsv, nwk, nwv, ng, mg) = jnp.split(h @ w_in, idx, axis=-1)

    def qk(t, heads, dims, g):
        return rope(rmsnorm(t.reshape((B, T, heads) + dims), g), pos)

    return {
        'dq': qk(dq, DIFF_HEADS, (2, DIFF_DH), diff_qk_norm[0]),
        'dk': qk(dk, DIFF_KV_HEADS, (2, DIFF_DH), diff_qk_norm[1]),
        'dv': dv.reshape(B, T, DIFF_KV_HEADS, DIFF_DV),
        'mq': qk(mq, MOBA_HEADS, (HEAD_DIM,), moba_qk_norm[0]),
        'mk': qk(mk, MOBA_KV_HEADS, (HEAD_DIM,), moba_qk_norm[1]),
        'mv': mv.reshape(B, T, MOBA_KV_HEADS, HEAD_DIM),
        'nq': qk(nq, NSA_HEADS, (HEAD_DIM,), nsa_qk_norm[0]),
        'nck': qk(nck, NSA_KV_HEADS, (HEAD_DIM,), nsa_qk_norm[1]),
        'ncv': ncv.reshape(B, T, NSA_KV_HEADS, HEAD_DIM),
        'nsk': qk(nsk, NSA_KV_HEADS, (HEAD_DIM,), nsa_qk_norm[2]),
        'nsv': nsv.reshape(B, T, NSA_KV_HEADS, HEAD_DIM),
        'nwk': qk(nwk, NSA_KV_HEADS, (HEAD_DIM,), nsa_qk_norm[3]),
        'nwv': nwv.reshape(B, T, NSA_KV_HEADS, HEAD_DIM),
        'ng': jax.nn.sigmoid(ng.reshape(B, T, NSA_HEADS, 3)),
        'mg': jax.nn.sigmoid(mg.reshape(B, T, N_BRANCH, D_MODEL)),
    }


def diff_core(q, q_pos, k, v, k_pos, lam, lam_init, out_norm):
    B, Cq = q.shape[:2]
    rep = DIFF_HEADS // DIFF_KV_HEADS
    qg = q.reshape(B, Cq, DIFF_KV_HEADS, rep, 2, DIFF_DH)
    s = jnp.einsum('bqgrcd,bkgcd->bgrcqk', qg, k).astype(jnp.float32) * (DIFF_DH ** -0.5)
    p = masked_softmax(s, k_pos[None, :] <= q_pos[:, None])
    a = p[:, :, :, 0] - lam * p[:, :, :, 1]
    o = jnp.einsum('bgrqk,bkgd->bqgrd', a.astype(v.dtype), v).reshape(B, Cq, DIFF_HEADS, DIFF_DV)
    o = rmsnorm(o, out_norm) * (1.0 - lam_init)
    return o.reshape(B, Cq, DIFF_WIDTH)


def moba_blocks(k, v):
    B, L, KV, D = k.shape
    nb = -(-L // MOBA_BLOCK)
    pad = ((0, 0), (0, nb * MOBA_BLOCK - L), (0, 0), (0, 0))
    kr = jnp.pad(k, pad).reshape(B, nb, MOBA_BLOCK, KV, D)
    vr = jnp.pad(v, pad).reshape(B, nb, MOBA_BLOCK, KV, D)
    kmean = jnp.mean(kr.astype(jnp.float32), axis=2).astype(k.dtype)
    return kr.transpose(0, 3, 1, 2, 4), vr.transpose(0, 3, 1, 2, 4), kmean


def moba_core(q, q_pos, kb, vb, kmean):
    B, Cq = q.shape[:2]
    G = MOBA_KV_HEADS
    rep = MOBA_HEADS // G
    nb = kb.shape[2]
    qg = q.reshape(B, Cq, G, rep, HEAD_DIM)
    own = q_pos // MOBA_BLOCK
    gate = jnp.einsum('bqgrd,bngd->bqgrn', qg, kmean).astype(jnp.float32)
    past_ok = (jnp.arange(nb)[None, :] < own[:, None])[None, :, None, None, :]
    gate = jnp.where(past_ok, gate, NEG)
    kk = min(MOBA_TOPK, nb)
    top_val, top_idx = lax.top_k(gate, kk)
    b_i = jnp.arange(B)[:, None, None, None, None]
    g_i = jnp.arange(G)[None, None, :, None, None]
    ks = kb[b_i, g_i, top_idx]
    vs = vb[b_i, g_i, top_idx]
    b_o = jnp.arange(B)[:, None, None]
    g_o = jnp.arange(G)[None, None, :]
    ko = kb[b_o, g_o, own[None, :, None]]
    vo = vb[b_o, g_o, own[None, :, None]]
    n_sel = kk * MOBA_BLOCK
    s_sel = jnp.einsum('bqgrd,bqgrnkd->bqgrnk', qg, ks).astype(jnp.float32).reshape(B, Cq, G, rep, n_sel)
    s_own = jnp.einsum('bqgrd,bqgkd->bqgrk', qg, ko).astype(jnp.float32)
    m_sel = jnp.broadcast_to((top_val > NEG / 2)[..., None], (B, Cq, G, rep, kk, MOBA_BLOCK)).reshape(B, Cq, G, rep, n_sel)
    own_pos = own[:, None] * MOBA_BLOCK + jnp.arange(MOBA_BLOCK)[None, :]
    m_own = jnp.broadcast_to((own_pos <= q_pos[:, None])[None, :, None, None, :], s_own.shape)
    pr = masked_softmax(jnp.concatenate([s_sel, s_own], axis=-1) * (HEAD_DIM ** -0.5),
                        jnp.concatenate([m_sel, m_own], axis=-1))
    p_sel = pr[..., :n_sel].reshape(B, Cq, G, rep, kk, MOBA_BLOCK).astype(vs.dtype)
    p_own = pr[..., n_sel:].astype(vo.dtype)
    o = jnp.einsum('bqgrnk,bqgrnkd->bqgrd', p_sel, vs) + jnp.einsum('bqgrk,bqgkd->bqgrd', p_own, vo)
    return o.reshape(B, Cq, MOBA_WIDTH)


def compress(rows, w1, w2, pos_emb):
    B, L, G, D = rows.shape
    n = -(-L // CMP_STRIDE)
    c = jnp.pad(rows, ((0, 0), (0, n * CMP_STRIDE - L), (0, 0), (0, 0))).reshape(B, n, CMP_STRIDE, G, D)
    pe = pos_emb.reshape(2, CMP_STRIDE, 1, D)
    w = w1.reshape(2, CMP_STRIDE, D, CMP_HIDDEN)
    hid = (jnp.einsum('bnjgd,jdh->bngh', c[:, :-1] + pe[0], w[0])
           + jnp.einsum('bnjgd,jdh->bngh', c[:, 1:] + pe[1], w[1]))
    return jax.nn.silu(hid) @ w2


def sel_blocks(rows):
    B, L, G, D = rows.shape
    ns = -(-L // SEL_BLOCK)
    r = jnp.pad(rows, ((0, 0), (0, ns * SEL_BLOCK - L), (0, 0), (0, 0))).reshape(B, ns, SEL_BLOCK, G, D)
    return r.transpose(0, 3, 1, 2, 4)


def block_importance(pc, ns):
    nc = pc.shape[-1]
    r_ = SEL_BLOCK // CMP_STRIDE
    f_ = CMP_BLOCK // CMP_STRIDE - 1
    pp = jnp.pad(pc, [(0, 0)] * (pc.ndim - 1) + [(f_, (ns + 1) * r_ - f_ - nc)])
    r = pp.reshape(pc.shape[:-1] + (ns + 1, r_))
    return jnp.sum(r[..., :ns, :], axis=-1) + jnp.sum(r[..., 1:, :f_], axis=-1)


def nsa_core(q, q_pos, gates, ck, cv, c_end, skb, svb, wk, wv, w_pos):
    B, Cq = q.shape[:2]
    G = NSA_KV_HEADS
    rep = NSA_HEADS // G
    scale = HEAD_DIM ** -0.5
    qg = q.reshape(B, Cq, G, rep, HEAD_DIM)
    s_c = jnp.einsum('bqgrd,bngd->bqgrn', qg, ck).astype(jnp.float32) * scale
    p_c = masked_softmax(s_c, (c_end[None, :] <= q_pos[:, None])[None, :, None, None, :])
    o_c = jnp.einsum('bqgrn,bngd->bqgrd', p_c.astype(cv.dtype), cv)
    ns = skb.shape[2]
    imp = block_importance(jnp.sum(p_c, axis=3), ns)
    cur = q_pos // SEL_BLOCK
    j = jnp.arange(ns)[None, :]
    valid = (j <= cur[:, None])[None, :, None, :]
    forced = ((j == 0) | (j == cur[:, None]) | (j == cur[:, None] - 1))[None, :, None, :]
    imp = jnp.where(valid, jnp.where(forced, FORCE_SCORE, imp), NEG)
    kk = min(SEL_TOPK, ns)
    tv, ti = lax.top_k(imp, kk)
    b_i = jnp.arange(B)[:, None, None, None]
    g_i = jnp.arange(G)[None, None, :, None]
    ks = skb[b_i, g_i, ti]
    vs = svb[b_i, g_i, ti]
    n_sel = kk * SEL_BLOCK
    pos_s = ti[..., None] * SEL_BLOCK + jnp.arange(SEL_BLOCK)
    m_s = ((tv > NEG / 2)[..., None] & (pos_s <= q_pos[None, :, None, None, None])).reshape(B, Cq, G, 1, n_sel)
    s_s = jnp.einsum('bqgrd,bqgnkd->bqgrnk', qg, ks).astype(jnp.float32).reshape(B, Cq, G, rep, n_sel) * scale
    p_s = masked_softmax(s_s, m_s)
    o_s = jnp.einsum('bqgrm,bqgmd->bqgrd', p_s.astype(vs.dtype), vs.reshape(B, Cq, G, n_sel, HEAD_DIM))
    s_w = jnp.einsum('bqgrd,bkgd->bqgrk', qg, wk).astype(jnp.float32) * scale
    dist = q_pos[:, None] - w_pos[None, :]
    m_w = ((dist >= 0) & (dist <= WINDOW) & (w_pos[None, :] >= 0))[None, :, None, None, :]
    p_w = masked_softmax(s_w, m_w)
    o_w = jnp.einsum('bqgrk,bkgd->bqgrd', p_w.astype(wv.dtype), wv)
    g = gates.reshape(B, Cq, G, rep, 3)
    o = g[..., 0:1] * o_c + g[..., 1:2] * o_s + g[..., 2:3] * o_w
    return o.reshape(B, Cq, NSA_WIDTH)


def merge(o_d, o_m, o_n, mg, w_branch, w_out):
    u = (mg[:, :, 0] * (o_d @ w_branch[:DIFF_WIDTH])
         + mg[:, :, 1] * (o_m @ w_branch[DIFF_WIDTH:DIFF_WIDTH + MOBA_WIDTH])
         + mg[:, :, 2] * (o_n @ w_branch[DIFF_WIDTH + MOBA_WIDTH:]))
    return u @ w_out


def mixer_prompt(h, lam, lam_init, w_in, diff_qk_norm, diff_out_norm, moba_qk_norm, nsa_qk_norm,
                 cmp_w1, cmp_w2, cmp_pos):
    B, T, _ = h.shape
    pos = jnp.arange(T, dtype=jnp.int32)
    p = project(h, pos, w_in, diff_qk_norm, moba_qk_norm, nsa_qk_norm)

    def blk(t, s, n):
        return lax.dynamic_slice_in_dim(t, s, n, axis=1)

    def diff_step(i):
        s = i * DIFF_Q_BLOCK
        return diff_core(blk(p['dq'], s, DIFF_Q_BLOCK), s + jnp.arange(DIFF_Q_BLOCK, dtype=jnp.int32),
                         p['dk'], p['dv'], pos, lam, lam_init, diff_out_norm)
    o_d = sweep(diff_step, T, DIFF_Q_BLOCK)

    kb, vb, kmean = moba_blocks(p['mk'], p['mv'])

    def moba_step(i):
        s = i * MOBA_Q_BLOCK
        return moba_core(blk(p['mq'], s, MOBA_Q_BLOCK), s + jnp.arange(MOBA_Q_BLOCK, dtype=jnp.int32), kb, vb, kmean)
    o_m = sweep(moba_step, T, MOBA_Q_BLOCK)

    ck = compress(p['nck'], cmp_w1[0], cmp_w2[0], cmp_pos[0])
    cv = compress(p['ncv'], cmp_w1[1], cmp_w2[1], cmp_pos[1])
    c_end = jnp.arange(ck.shape[1], dtype=jnp.int32) * CMP_STRIDE + (CMP_BLOCK - 1)
    skb = sel_blocks(p['nsk'])
    svb = sel_blocks(p['nsv'])
    pad = ((0, 0), (WINDOW, 0), (0, 0), (0, 0))
    wkp = jnp.pad(p['nwk'], pad)
    wvp = jnp.pad(p['nwv'], pad)
    lw = WINDOW + NSA_Q_BLOCK

    def nsa_step(i):
        s = i * NSA_Q_BLOCK
        return nsa_core(blk(p['nq'], s, NSA_Q_BLOCK), s + jnp.arange(NSA_Q_BLOCK, dtype=jnp.int32),
                        blk(p['ng'], s, NSA_Q_BLOCK), ck, cv, c_end, skb, svb,
                        blk(wkp, s, lw), blk(wvp, s, lw), s - WINDOW + jnp.arange(lw, dtype=jnp.int32))
    o_n = sweep(nsa_step, T, NSA_Q_BLOCK)

    rows_d = jnp.stack([p['dk'].reshape(B, T, DIFF_KV_HEADS, HEAD_DIM), p['dv']], axis=2)
    rows_m = jnp.stack([p['mk'], p['mv']], axis=2)
    rows_n = jnp.stack([p['nck'], p['ncv'], p['nsk'], p['nsv']], axis=2)
    win = jnp.stack([p['nwk'], p['nwv']], axis=2)[:, T - min(WINDOW, T):]
    return o_d, o_m, o_n, p['mg'], rows_d, rows_m, rows_n, win


def mixer_sample(h, l, cache_diff, cache_moba, cache_nsa, state_nsa_win, page_table, lam, lam_init,
                 w_in, diff_qk_norm, diff_out_norm, moba_qk_norm, nsa_qk_norm, cmp_w1, cmp_w2, cmp_pos):
    B, T, _ = h.shape
    pos = PAST_LEN + jnp.arange(T, dtype=jnp.int32)
    p = project(h, pos, w_in, diff_qk_norm, moba_qk_norm, nsa_qk_norm)
    k_pos = jnp.arange(PAST_LEN + T, dtype=jnp.int32)

    def past(cache):
        g = cache[l, page_table]
        return g.reshape((B, PAST_LEN) + cache.shape[3:])

    pd = past(cache_diff)
    dk_all = jnp.concatenate([pd[:, :, 0].reshape(B, PAST_LEN, DIFF_KV_HEADS, 2, DIFF_DH), p['dk']], axis=1)
    dv_all = jnp.concatenate([pd[:, :, 1], p['dv']], axis=1)
    o_d = diff_core(p['dq'], pos, dk_all, dv_all, k_pos, lam, lam_init, diff_out_norm)

    pm = past(cache_moba)
    kb, vb, kmean = moba_blocks(jnp.concatenate([pm[:, :, 0], p['mk']], axis=1),
                                jnp.concatenate([pm[:, :, 1], p['mv']], axis=1))
    o_m = moba_core(p['mq'], pos, kb, vb, kmean)

    pn = past(cache_nsa)
    ck = compress(jnp.concatenate([pn[:, :, 0], p['nck']], axis=1), cmp_w1[0], cmp_w2[0], cmp_pos[0])
    cv = compress(jnp.concatenate([pn[:, :, 1], p['ncv']], axis=1), cmp_w1[1], cmp_w2[1], cmp_pos[1])
    c_end = jnp.arange(ck.shape[1], dtype=jnp.int32) * CMP_STRIDE + (CMP_BLOCK - 1)
    skb = sel_blocks(jnp.concatenate([pn[:, :, 2], p['nsk']], axis=1))
    svb = sel_blocks(jnp.concatenate([pn[:, :, 3], p['nsv']], axis=1))
    ws = state_nsa_win[l]
    wb = ws.shape[1]
    wk = jnp.concatenate([ws[:, :, 0], p['nwk']], axis=1)
    wv = jnp.concatenate([ws[:, :, 1], p['nwv']], axis=1)
    w_pos = PAST_LEN - wb + jnp.arange(wb + T, dtype=jnp.int32)
    o_n = nsa_core(p['nq'], pos, p['ng'], ck, cv, c_end, skb, svb, wk, wv, w_pos)

    rows_d = jnp.stack([p['dk'].reshape(B, T, DIFF_KV_HEADS, HEAD_DIM), p['dv']], axis=2)
    rows_m = jnp.stack([p['mk'], p['mv']], axis=2)
    rows_n = jnp.stack([p['nck'], p['ncv'], p['nsk'], p['nsv']], axis=2)
    win = jnp.stack([wk, wv], axis=2)[:, T:]
    return o_d, o_m, o_n, p['mg'], rows_d, rows_m, rows_n, win


def setup_inputs(seed: int = 0) -> dict:
    key = jax.random.key(seed)
    ks = jax.random.split(key, 32)
    n_pages = PAST_LEN // PAGE_SIZE
    in_use = DEC_BATCH * n_pages
    n_pool = in_use + max(1, in_use // 4)
    win_buf = min(WINDOW, PAST_LEN)

    def nrm(k, shape, scale=1.0):
        return scale * jax.random.normal(k, shape, jnp.float32)

    def gain(k, shape):
        return 1.0 + 0.02 * jax.random.normal(k, shape, jnp.float32)

    perm = jax.random.permutation(ks[6], n_pool)
    page_table = perm[:in_use].reshape(DEC_BATCH, n_pages).astype(jnp.int32)
    return {
        'x_prompt': nrm(ks[0], (BATCH, SEQ, D_MODEL)),
        'x_sample': nrm(ks[1], (DEC_BATCH, DEC_SEQ, D_MODEL)),
        'cache_diff': nrm(ks[2], (DEPTH, n_pool, PAGE_SIZE, 2, DIFF_KV_HEADS, HEAD_DIM)),
        'cache_moba': nrm(ks[3], (DEPTH, n_pool, PAGE_SIZE, 2, MOBA_KV_HEADS, HEAD_DIM)),
        'cache_nsa': nrm(ks[4], (DEPTH, n_pool, PAGE_SIZE, 4, NSA_KV_HEADS, HEAD_DIM)),
        'state_nsa_win': nrm(ks[5], (DEPTH, DEC_BATCH, win_buf, 2, NSA_KV_HEADS, HEAD_DIM)),
        'page_table': page_table,
        'ffn1_norm': gain(ks[7], (DEPTH, D_MODEL)),
        'ffn1_w_gu': nrm(ks[8], (DEPTH, D_MODEL, 2 * D_FF), D_MODEL ** -0.5),
        'ffn1_w_down': nrm(ks[9], (DEPTH, D_FF, D_MODEL), D_FF ** -0.5),
        'mix_norm': gain(ks[10], (DEPTH, D_MODEL)),
        'w_in': nrm(ks[11], (DEPTH, D_MODEL, IN_COLS), D_MODEL ** -0.5),
        'diff_qk_norm': gain(ks[12], (DEPTH, 2, DIFF_DH)),
        'diff_lambda': nrm(ks[13], (DEPTH, 4, DIFF_DH), 0.1),
        'diff_out_norm': gain(ks[14], (DEPTH, DIFF_DV)),
        'moba_qk_norm': gain(ks[15], (DEPTH, 2, HEAD_DIM)),
        'nsa_qk_norm': gain(ks[16], (DEPTH, 4, HEAD_DIM)),
        'nsa_cmp_w1': nrm(ks[17], (DEPTH, 2, CMP_BLOCK, HEAD_DIM, CMP_HIDDEN), (CMP_BLOCK * HEAD_DIM) ** -0.5),
        'nsa_cmp_w2': nrm(ks[18], (DEPTH, 2, CMP_HIDDEN, HEAD_DIM), CMP_HIDDEN ** -0.5),
        'nsa_cmp_pos': nrm(ks[19], (DEPTH, 2, CMP_BLOCK, HEAD_DIM), 0.1),
        'w_branch': nrm(ks[20], (DEPTH, MIX_WIDTH, D_MODEL), DIFF_WIDTH ** -0.5),
        'w_out': nrm(ks[21], (DEPTH, D_MODEL, D_MODEL), D_MODEL ** -0.5),
        'ffn2_norm': gain(ks[22], (DEPTH, D_MODEL)),
        'ffn2_w_gu': nrm(ks[23], (DEPTH, D_MODEL, 2 * D_FF), D_MODEL ** -0.5),
        'ffn2_w_down': nrm(ks[24], (DEPTH, D_FF, D_MODEL), D_FF ** -0.5),
    }


def reference(x_prompt, x_sample, cache_diff, cache_moba, cache_nsa, state_nsa_win, page_table,
              ffn1_norm, ffn1_w_gu, ffn1_w_down, mix_norm, w_in, diff_qk_norm, diff_lambda, diff_out_norm,
              moba_qk_norm, nsa_qk_norm, nsa_cmp_w1, nsa_cmp_w2, nsa_cmp_pos, w_branch, w_out,
              ffn2_norm, ffn2_w_gu, ffn2_w_down):
    xp = x_prompt
    xs = x_sample
    dp, ds, mp, ms, npr, nsm, wp, wsm = [], [], [], [], [], [], [], []
    for l in range(DEPTH):
        lam_init = 0.8 - 0.6 * math.exp(-0.3 * l)
        lq = diff_lambda[l].astype(jnp.float32)
        lam = jnp.exp(jnp.sum(lq[0] * lq[1])) - jnp.exp(jnp.sum(lq[2] * lq[3])) + lam_init
        xp = xp + 0.5 * swiglu_half(xp, ffn1_norm[l], ffn1_w_gu[l], ffn1_w_down[l])
        xs = xs + 0.5 * swiglu_half(xs, ffn1_norm[l], ffn1_w_gu[l], ffn1_w_down[l])
        o_d, o_m, o_n, mg, rd, rm, rn, wn = mixer_prompt(
            rmsnorm(xp, mix_norm[l]), lam, lam_init, w_in[l], diff_qk_norm[l], diff_out_norm[l],
            moba_qk_norm[l], nsa_qk_norm[l], nsa_cmp_w1[l], nsa_cmp_w2[l], nsa_cmp_pos[l])
        xp = xp + merge(o_d, o_m, o_n, mg, w_branch[l], w_out[l])
        dp.append(rd)
        mp.append(rm)
        npr.append(rn)
        wp.append(wn)
        o_d, o_m, o_n, mg, rd, rm, rn, wn = mixer_sample(
            rmsnorm(xs, mix_norm[l]), l, cache_diff, cache_moba, cache_nsa, state_nsa_win, page_table,
            lam, lam_init, w_in[l], diff_qk_norm[l], diff_out_norm[l], moba_qk_norm[l], nsa_qk_norm[l],
            nsa_cmp_w1[l], nsa_cmp_w2[l], nsa_cmp_pos[l])
        xs = xs + merge(o_d, o_m, o_n, mg, w_branch[l], w_out[l])
        ds.append(rd)
        ms.append(rm)
        nsm.append(rn)
        wsm.append(wn)
        xp = xp + 0.5 * swiglu_half(xp, ffn2_norm[l], ffn2_w_gu[l], ffn2_w_down[l])
        xs = xs + 0.5 * swiglu_half(xs, ffn2_norm[l], ffn2_w_gu[l], ffn2_w_down[l])
    return (xp, xs, jnp.stack(dp), jnp.stack(ds), jnp.stack(mp), jnp.stack(ms),
            jnp.stack(npr), jnp.stack(nsm), jnp.stack(wp), jnp.stack(wsm))
```

```python
import functools
import math

import jax
import jax.numpy as jnp
import numpy as np
from jax import lax
from jax.experimental import pallas as pl
from jax.experimental.pallas import tpu as pltpu

F32 = jnp.float32
BF16 = jnp.bfloat16
I32 = jnp.int32

D_MODEL = 4096
D_FF = 11008
HEAD_DIM = 128
PAGE_SIZE = 128
ROPE_THETA = 10000.0
EPS = 1e-6
NEG = -1e30
TINY = 1e-30
DIFF_HEADS, DIFF_KV_HEADS, DIFF_DH = 8, 4, 64
MOBA_HEADS, MOBA_KV_HEADS, MOBA_BLOCK, MOBA_TOPK = 8, 4, 256, 3
NSA_HEADS, NSA_KV_HEADS = 16, 2
NSA_REP = NSA_HEADS // NSA_KV_HEADS
CMP_BLOCK, CMP_STRIDE, CMP_HIDDEN = 32, 16, 256
SEL_BLOCK, SEL_TOPK, WINDOW = 64, 16, 512
FORCE_SCORE = 1e4
QKV_COLS = 7680
NG_COLS = NSA_HEADS * 3
NG_PAD = NSA_KV_HEADS * 128
MG_COLS = 3 * D_MODEL

LANES = 128
VMEM_LIMIT_BYTES = 56 * 1024 * 1024
SCALE128 = HEAD_DIM ** -0.5


def _cparams(*sem):
    return pltpu.CompilerParams(dimension_semantics=sem, vmem_limit_bytes=VMEM_LIMIT_BYTES)


def _dot(a, b):
    return jnp.dot(a, b, preferred_element_type=F32)


def _dot_nt(a, b):
    return lax.dot_general(a, b, (((1,), (1,)), ((), ())), preferred_element_type=F32)


def _dot_exact(a, b):
    return jnp.dot(a, b, preferred_element_type=F32, precision=lax.Precision.HIGHEST)


def _rmsnorm_kernel(x_ref, g_ref, o_ref):
    x = x_ref[...]
    ms = jnp.mean(x * x, axis=-1, keepdims=True)
    o_ref[...] = (x * lax.rsqrt(ms + EPS) * g_ref[...]).astype(o_ref.dtype)


def rmsnorm_bf16(x, g):
    M, D = x.shape
    tm = min(M, 256)
    return pl.pallas_call(
        _rmsnorm_kernel, grid=(M // tm,),
        in_specs=[pl.BlockSpec((tm, D), lambda i: (i, 0)), pl.BlockSpec((1, D), lambda i: (0, 0))],
        out_specs=pl.BlockSpec((tm, D), lambda i: (i, 0)),
        out_shape=jax.ShapeDtypeStruct((M, D), BF16),
        compiler_params=_cparams("parallel"), name="rmsnorm",
    )(x, g.reshape(1, D))


def _mm_kernel(a_ref, b_ref, o_ref, *, sigmoid):
    acc = _dot(a_ref[...], b_ref[...])
    if sigmoid:
        acc = jax.nn.sigmoid(acc)
    o_ref[...] = acc.astype(o_ref.dtype)


def matmul(a, b, *, tn, out_dtype=F32, sigmoid=False, name="matmul"):
    M, K = a.shape
    N = b.shape[1]
    tm = min(M, 1024)
    return pl.pallas_call(
        functools.partial(_mm_kernel, sigmoid=sigmoid), grid=(M // tm, N // tn),
        in_specs=[pl.BlockSpec((tm, K), lambda i, j: (i, 0)), pl.BlockSpec((K, tn), lambda i, j: (0, j))],
        out_specs=pl.BlockSpec((tm, tn), lambda i, j: (i, j)),
        out_shape=jax.ShapeDtypeStruct((M, N), out_dtype),
        compiler_params=_cparams("parallel", "parallel"), name=name,
    )(a, b)


def _mm_swiglu_kernel(a_ref, bg_ref, bu_ref, o_ref):
    a = a_ref[...]
    g = _dot(a, bg_ref[...])
    u = _dot(a, bu_ref[...])
    o_ref[...] = (g * jax.nn.sigmoid(g) * u).astype(o_ref.dtype)


def matmul_swiglu(a, w_gu, *, tn=256):
    M, K = a.shape
    F = w_gu.shape[1] // 2
    tm = min(M, 1024)
    nf = F // tn
    return pl.pallas_call(
        _mm_swiglu_kernel, grid=(M // tm, nf),
        in_specs=[pl.BlockSpec((tm, K), lambda i, j: (i, 0)),
                  pl.BlockSpec((K, tn), lambda i, j: (0, j)),
                  pl.BlockSpec((K, tn), lambda i, j: (0, j + nf))],
        out_specs=pl.BlockSpec((tm, tn), lambda i, j: (i, j)),
        out_shape=jax.ShapeDtypeStruct((M, F), BF16),
        compiler_params=_cparams("parallel", "parallel"), name="ffn_gate_up",
    )(a, w_gu, w_gu)


def _mm_res_kernel(a_ref, b_ref, x_ref, o_ref, *, alpha):
    o_ref[...] = x_ref[...] + alpha * _dot(a_ref[...], b_ref[...])


def matmul_residual(a, b, x, alpha, *, tm, tn, name):
    M, K = a.shape
    N = b.shape[1]
    tm = min(M, tm)
    return pl.pallas_call(
        functools.partial(_mm_res_kernel, alpha=alpha), grid=(M // tm, N // tn),
        in_specs=[pl.BlockSpec((tm, K), lambda i, j: (i, 0)), pl.BlockSpec((K, tn), lambda i, j: (0, j)),
                  pl.BlockSpec((tm, tn), lambda i, j: (i, j))],
        out_specs=pl.BlockSpec((tm, tn), lambda i, j: (i, j)),
        out_shape=jax.ShapeDtypeStruct((M, N), F32),
        compiler_params=_cparams("parallel", "parallel"), name=name,
    )(a, b, x)


def _mm_merge_kernel(od_ref, om_ref, on_ref, wd_ref, wm_ref, wn_ref, g0_ref, g1_ref, g2_ref, o_ref):
    u = g0_ref[...] * _dot(od_ref[...], wd_ref[...])
    u = u + g1_ref[...] * _dot(om_ref[...], wm_ref[...])
    u = u + g2_ref[...] * _dot(on_ref[...], wn_ref[...])
    o_ref[...] = u.astype(o_ref.dtype)


def matmul_merge(o_d, o_m, o_n, w_branch, mg, *, tn=512):
    M = o_d.shape[0]
    D = w_branch.shape[1]
    tm = min(M, 1024)
    wd, wm, wn = o_d.shape[1], o_m.shape[1], o_n.shape[1]
    nd = D // tn
    return pl.pallas_call(
        _mm_merge_kernel, grid=(M // tm, nd),
        in_specs=[pl.BlockSpec((tm, wd), lambda i, j: (i, 0)),
                  pl.BlockSpec((tm, wm), lambda i, j: (i, 0)),
                  pl.BlockSpec((tm, wn), lambda i, j: (i, 0)),
                  pl.BlockSpec((wd, tn), lambda i, j: (0, j)),
                  pl.BlockSpec((wm, tn), lambda i, j: (1, j)),
                  pl.BlockSpec((wn, tn), lambda i, j: (1, j)),
                  pl.BlockSpec((tm, tn), lambda i, j: (i, j)),
                  pl.BlockSpec((tm, tn), lambda i, j: (i, j + nd)),
                  pl.BlockSpec((tm, tn), lambda i, j: (i, j + 2 * nd))],
        out_specs=pl.BlockSpec((tm, tn), lambda i, j: (i, j)),
        out_shape=jax.ShapeDtypeStruct((M, D), BF16),
        compiler_params=_cparams("parallel", "parallel"), name="merge",
    )(o_d, o_m, o_n, w_branch, w_branch, w_branch, mg, mg, mg)


def _post_kernel(p_ref, c128_ref, s128_ref, c64_ref, s64_ref, gd_ref, gm_ref, gn_ref,
                 dq_ref, rd_ref, rdb_ref, mq_ref, rm_ref, rmb_ref, nq_ref, rn_ref, rnb_ref,
                 win_ref, winb_ref, *rest, tm, emit_chunks):
    lane = lax.broadcasted_iota(I32, (tm, LANES), 1)
    lo = lane < 64
    first_half64 = (lane & 63) < 32
    c128, s128 = c128_ref[...], s128_ref[...]
    c64, s64 = c64_ref[...], s64_ref[...]

    def slab(k):
        return p_ref[:, k * LANES:(k + 1) * LANES]

    def nr128(x, g):
        ms = jnp.mean(x * x, axis=-1, keepdims=True)
        y = x * lax.rsqrt(ms + EPS) * g
        return y * c128 + pltpu.roll(y, 64, 1) * s128

    def nr64(x, g):
        x2 = x * x
        s_lo = jnp.sum(jnp.where(lo, x2, 0.0), axis=-1, keepdims=True)
        s_hi = jnp.sum(jnp.where(lo, 0.0, x2), axis=-1, keepdims=True)
        ms = jnp.where(lo, s_lo, s_hi) * (1.0 / 64.0)
        y = x * lax.rsqrt(ms + EPS) * g
        partner = jnp.where(first_half64, pltpu.roll(y, 96, 1), pltpu.roll(y, 32, 1))
        return y * c64 + partner * s64

    def put(ref_f, ref_b, k, val):
        if ref_f is not None:
            ref_f[:, k * LANES:(k + 1) * LANES] = val
        if ref_b is not None:
            ref_b[:, k * LANES:(k + 1) * LANES] = val.astype(BF16)

    gdq, gdk = gd_ref[0:1, :], gd_ref[1:2, :]
    gmq, gmk = gm_ref[0:1, :], gm_ref[1:2, :]
    for h in range(8):
        put(None, dq_ref, h, nr64(slab(h), gdq) * 0.125)
    for h in range(4):
        put(rd_ref, rdb_ref, h, nr64(slab(8 + h), gdk))
        put(rd_ref, rdb_ref, 4 + h, slab(12 + h))
    for h in range(8):
        put(None, mq_ref, h, nr128(slab(16 + h), gmq))
    for h in range(4):
        put(rm_ref, rmb_ref, h, nr128(slab(24 + h), gmk))
        put(rm_ref, rmb_ref, 4 + h, slab(28 + h))
    for h in range(16):
        put(None, nq_ref, h, nr128(slab(32 + h), gn_ref[0:1, :]))
    for g in range(2):
        put(rn_ref, rnb_ref, g, nr128(slab(48 + g), gn_ref[1:2, :]))
        put(rn_ref, rnb_ref, 2 + g, slab(50 + g))
        put(rn_ref, rnb_ref, 4 + g, nr128(slab(52 + g), gn_ref[2:3, :]))
        put(rn_ref, rnb_ref, 6 + g, slab(54 + g))
        put(win_ref, winb_ref, g, nr128(slab(56 + g), gn_ref[3:4, :]))
        put(win_ref, winb_ref, 2 + g, slab(58 + g))
    if emit_chunks:
        xc_ref, slab_ref = rest
        n = tm // CMP_STRIDE
        for cg in range(4):
            slab_ref[...] = rn_ref[:, cg * LANES:(cg + 1) * LANES]
            for j in range(CMP_STRIDE):
                xc_ref[0, cg, :, j * LANES:(j + 1) * LANES] = slab_ref[pl.ds(j, n, stride=CMP_STRIDE), :]


def _rope_tables(pos):
    def tab(d):
        inv = ROPE_THETA ** (-jnp.arange(0, d, 2, dtype=F32) / d)
        ang = pos.astype(F32)[:, None] * inv[None, :]
        cos, sin = jnp.cos(ang), jnp.sin(ang)
        c = jnp.concatenate([cos, cos], axis=-1)
        s = jnp.concatenate([-sin, sin], axis=-1)
        rep = LANES // d
        return jnp.tile(c, (1, rep)), jnp.tile(s, (1, rep))
    c128, s128 = tab(128)
    c64, s64 = tab(64)
    return c128, s128, c64, s64


def post_project(proj, tables, gd, gm, gn, *, seq, batch):
    M = proj.shape[0]
    tm = min(M, 256)
    emit_chunks = seq >= tm
    nt = max(seq // tm, 1)
    row = lambda w: pl.BlockSpec((tm, w), lambda i: (i, 0))
    tab = pl.BlockSpec((tm, LANES), lambda i: (i % nt, 0))
    par = lambda r: pl.BlockSpec((r, LANES), lambda i: (0, 0))
    out_shapes = [
        jax.ShapeDtypeStruct((M, 1024), BF16),
        jax.ShapeDtypeStruct((M, 1024), F32), jax.ShapeDtypeStruct((M, 1024), BF16),
        jax.ShapeDtypeStruct((M, 1024), BF16),
        jax.ShapeDtypeStruct((M, 1024), F32), jax.ShapeDtypeStruct((M, 1024), BF16),
        jax.ShapeDtypeStruct((M, 2048), BF16),
        jax.ShapeDtypeStruct((M, 1024), F32), jax.ShapeDtypeStruct((M, 1024), BF16),
        jax.ShapeDtypeStruct((M, 512), F32), jax.ShapeDtypeStruct((M, 512), BF16),
    ]
    out_specs = [row(1024), row(1024), row(1024), row(1024), row(1024), row(1024), row(2048),
                 row(1024), row(1024), row(512), row(512)]
    if emit_chunks:
        nchunk = seq // CMP_STRIDE
        out_shapes.append(jax.ShapeDtypeStruct((batch, 4, nchunk, CMP_STRIDE * LANES), F32))
        out_specs.append(pl.BlockSpec((1, 4, tm // CMP_STRIDE, CMP_STRIDE * LANES),
                                      lambda i: (i // nt, 0, i % nt, 0)))
    return pl.pallas_call(
        functools.partial(_post_kernel, tm=tm, emit_chunks=emit_chunks), grid=(M // tm,),
        in_specs=[row(QKV_COLS), tab, tab, tab, tab, par(2), par(2), par(4)],
        out_specs=out_specs, out_shape=out_shapes,
        scratch_shapes=[pltpu.VMEM((tm, LANES), F32)] if emit_chunks else [],
        compiler_params=_cparams("parallel"), name="post_project",
    )(proj, *tables, gd, gm, gn)


def _online(carry, s, v, ok=None):
    m, l, acc = carry
    m_new = jnp.maximum(m, jnp.max(s, axis=-1, keepdims=True))
    alpha = jnp.exp(m - m_new)
    p = jnp.exp(s - m_new)
    if ok is not None:
        p = jnp.where(ok, p, 0.0)
    l = alpha * l + jnp.sum(p, axis=-1, keepdims=True)
    acc = alpha * acc + _dot(p.astype(BF16), v)
    return m_new, l, acc


def _init_carry(rows, width):
    return (jnp.full((rows, 1), NEG, F32), jnp.zeros((rows, 1), F32), jnp.zeros((rows, width), F32))


def _lambda(lam_ref, lam_init):
    lq = lam_ref[...]
    a = jnp.sum(lq[0:1, :] * lq[1:2, :], axis=-1, keepdims=True)
    b = jnp.sum(lq[2:3, :] * lq[3:4, :], axis=-1, keepdims=True)
    return jnp.exp(a) - jnp.exp(b) + lam_init


def _head_norm(o, g, post):
    ms = jnp.mean(o * o, axis=-1, keepdims=True)
    return o * lax.rsqrt(ms + EPS) * g * post


def _diff_prompt_kernel(lam_ref, gn_ref, q_ref, k_ref, v_ref, o_ref, *, tq, lam_init):
    i = pl.program_id(2)
    lane = lax.broadcasted_iota(I32, (tq, LANES), 1)
    q = q_ref[...]
    zero = jnp.zeros((tq, LANES), BF16)
    parts = []
    for r in range(2):
        qr = q[:, r * LANES:(r + 1) * LANES]
        parts.append(jnp.where(lane < 64, qr, zero))
        parts.append(jnp.where(lane < 64, zero, qr))
    Q = jnp.concatenate(parts, axis=0)
    R = 4 * tq

    def blk(kb):
        s0 = pl.multiple_of(kb * tq, tq)
        return k_ref[pl.ds(s0, tq), :], v_ref[pl.ds(s0, tq), :]

    def body(kb, carry):
        k, v = blk(kb)
        return _online(carry, _dot_nt(Q, k), v)

    carry = lax.fori_loop(0, i, body, _init_carry(R, LANES))
    k, v = blk(i)
    s = _dot_nt(Q, k)
    rowq = lax.broadcasted_iota(I32, (R, tq), 0) & (tq - 1)
    col = lax.broadcasted_iota(I32, (R, tq), 1)
    s = jnp.where(col <= rowq, s, NEG)
    m, l, acc = _online(carry, s, v)
    o = acc / jnp.maximum(l, TINY)
    lam = _lambda(lam_ref, lam_init)
    g = gn_ref[...]
    for r in range(2):
        o_r = o[(2 * r) * tq:(2 * r + 1) * tq] - lam * o[(2 * r + 1) * tq:(2 * r + 2) * tq]
        o_ref[:, r * LANES:(r + 1) * LANES] = _head_norm(o_r, g, 1.0 - lam_init).astype(o_ref.dtype)


def diff_prompt(dq, rows_b, lam_param, out_norm, lam_init, *, batch, seq):
    M = dq.shape[0]
    tq = 256
    nq = seq // tq
    return pl.pallas_call(
        functools.partial(_diff_prompt_kernel, tq=tq, lam_init=lam_init),
        grid=(batch, DIFF_KV_HEADS, nq),
        in_specs=[pl.BlockSpec((4, DIFF_DH), lambda b, g, i: (0, 0)),
                  pl.BlockSpec((1, LANES), lambda b, g, i: (0, 0)),
                  pl.BlockSpec((tq, 2 * LANES), lambda b, g, i: (b * nq + i, g)),
                  pl.BlockSpec((seq, LANES), lambda b, g, i: (b, g)),
                  pl.BlockSpec((seq, LANES), lambda b, g, i: (b, DIFF_KV_HEADS + g))],
        out_specs=pl.BlockSpec((tq, 2 * LANES), lambda b, g, i: (b * nq + i, g)),
        out_shape=jax.ShapeDtypeStruct((M, DIFF_HEADS * LANES), BF16),
        compiler_params=_cparams("parallel", "parallel", "parallel"), name="diff_prompt",
    )(lam_param, out_norm.reshape(1, LANES), dq, rows_b, rows_b)


def _rank_lt(vals, n_cand, lane):
    cnt = jnp.zeros(vals.shape, F32)
    for m in range(n_cand):
        col = vals[:, m:m + 1]
        beats = (col > vals) | ((col == vals) & (lane > m))
        cnt = cnt + jnp.where(beats, 1.0, 0.0)
    return cnt


def _moba_prompt_kernel(q_ref, kf_ref, k_ref, v_ref, o_ref, *, tq, nb):
    i = pl.program_id(2)
    q = q_ref[...]
    Q = jnp.concatenate([q[:, :LANES], q[:, LANES:]], axis=0)
    R = 2 * tq
    lane = lax.broadcasted_iota(I32, (R, LANES), 1)
    kms = [jnp.mean(kf_ref[n * MOBA_BLOCK:(n + 1) * MOBA_BLOCK, :], axis=0, keepdims=True) for n in range(nb)]
    kmean = jnp.concatenate(kms + [jnp.zeros((LANES - nb, LANES), F32)], axis=0)
    gate = _dot_nt(Q, kmean.astype(BF16))
    gate = jnp.where(lane < i, gate, NEG)
    cnt = _rank_lt(gate, nb, lane)
    sel = jnp.where((cnt < MOBA_TOPK) & (gate > NEG / 2), 1.0, 0.0)

    def blk(kb):
        s0 = pl.multiple_of(kb * tq, tq)
        return k_ref[pl.ds(s0, tq), :], v_ref[pl.ds(s0, tq), :]

    def body(kb, carry):
        k, v = blk(kb)
        on = jnp.sum(jnp.where(lane == kb, sel, 0.0), axis=-1, keepdims=True) > 0.5
        s = jnp.where(on, _dot_nt(Q, k) * SCALE128, NEG)
        return _online(carry, s, v, ok=on)

    carry = lax.fori_loop(0, i, body, _init_carry(R, LANES))
    k, v = blk(i)
    rowq = lax.broadcasted_iota(I32, (R, tq), 0) & (tq - 1)
    col = lax.broadcasted_iota(I32, (R, tq), 1)
    s = jnp.where(col <= rowq, _dot_nt(Q, k) * SCALE128, NEG)
    m, l, acc = _online(carry, s, v)
    o = acc / jnp.maximum(l, TINY)
    for r in range(2):
        o_ref[:, r * LANES:(r + 1) * LANES] = o[r * tq:(r + 1) * tq].astype(o_ref.dtype)


def moba_prompt(mq, rows_f, rows_b, *, batch, seq):
    M = mq.shape[0]
    tq = MOBA_BLOCK
    nq = seq // tq
    return pl.pallas_call(
        functools.partial(_moba_prompt_kernel, tq=tq, nb=seq // MOBA_BLOCK),
        grid=(batch, MOBA_KV_HEADS, nq),
        in_specs=[pl.BlockSpec((tq, 2 * LANES), lambda b, g, i: (b * nq + i, g)),
                  pl.BlockSpec((seq, LANES), lambda b, g, i: (b, g)),
                  pl.BlockSpec((seq, LANES), lambda b, g, i: (b, g)),
                  pl.BlockSpec((seq, LANES), lambda b, g, i: (b, MOBA_KV_HEADS + g))],
        out_specs=pl.BlockSpec((tq, 2 * LANES), lambda b, g, i: (b * nq + i, g)),
        out_shape=jax.ShapeDtypeStruct((M, MOBA_HEADS * LANES), BF16),
        compiler_params=_cparams("parallel", "parallel", "parallel"), name="moba_prompt",
    )(mq, rows_f, rows_b, rows_b)


def _compress_kernel(x_ref, pe_ref, w0_ref, w1_ref, w2_ref, o_ref, *, n):
    x = x_ref[0, 0]
    y0 = _dot((x + pe_ref[0, 0:1, :]).astype(BF16), w0_ref[0])
    y1 = _dot((x + pe_ref[0, 1:2, :]).astype(BF16), w1_ref[0])
    hid = y0 + pltpu.roll(y1, n - 1, 0)
    act = hid * jax.nn.sigmoid(hid)
    o_ref[0, 0] = _dot(act.astype(BF16), w2_ref[0]).astype(o_ref.dtype)


def compress(xc, pe, w0, w1, w2):
    B, _, n, W = xc.shape
    return pl.pallas_call(
        functools.partial(_compress_kernel, n=n), grid=(B, 4),
        in_specs=[pl.BlockSpec((1, 1, n, W), lambda b, c: (b, c, 0, 0)),
                  pl.BlockSpec((1, 2, W), lambda b, c: (c // 2, 0, 0)),
                  pl.BlockSpec((1, W, CMP_HIDDEN), lambda b, c: (c // 2, 0, 0)),
                  pl.BlockSpec((1, W, CMP_HIDDEN), lambda b, c: (c // 2, 0, 0)),
                  pl.BlockSpec((1, CMP_HIDDEN, LANES), lambda b, c: (c // 2, 0, 0))],
        out_specs=pl.BlockSpec((1, 1, n, LANES), lambda b, c: (b, c, 0, 0)),
        out_shape=jax.ShapeDtypeStruct((B, 4, n, LANES), BF16),
        compiler_params=_cparams("parallel", "parallel"), name="compress",
    )(xc, pe, w0, w1, w2)


def _nsa_prompt_kernel(q_ref, g_ref, ck_ref, cv_ref, sk_ref, sv_ref, wk_ref, wv_ref, imp_ref, exp_ref,
                       o_ref, msel_ref, *, tq, seq):
    i = pl.program_id(2)
    H = NSA_REP
    R = H * tq
    n_cmp = seq // CMP_STRIDE - 1
    n_sel = seq // SEL_BLOCK
    q = q_ref[...]
    Q = jnp.concatenate([q[:, r * LANES:(r + 1) * LANES] for r in range(H)], axis=0)
    lane3 = lax.broadcasted_iota(I32, (1, tq, LANES), 2)
    qp3 = i * tq + lax.broadcasted_iota(I32, (1, tq, LANES), 1)

    s = (_dot_nt(Q, ck_ref[0, 0]) * SCALE128).reshape(H, tq, LANES)
    ok = (lane3 < n_cmp) & (lane3 * CMP_STRIDE + (CMP_BLOCK - 1) <= qp3)
    s = jnp.where(ok, s, NEG)
    p = jnp.where(ok, jnp.exp(s - jnp.max(s, axis=-1, keepdims=True)), 0.0)
    pc = p / jnp.maximum(jnp.sum(p, axis=-1, keepdims=True), TINY)
    o_c = _dot(pc.reshape(R, LANES).astype(BF16), cv_ref[0, 0])

    pcs = jnp.sum(pc, axis=0)
    imp = _dot_exact(pcs, imp_ref[...])
    lane = lane3[0]
    cur = qp3[0] >> 6
    valid = lane <= cur
    forced = (lane == 0) | (lane == cur) | (lane == cur - 1)
    imp = jnp.where(valid, jnp.where(forced, FORCE_SCORE, imp), NEG)
    cnt = _rank_lt(imp, n_sel, lane)
    sel = jnp.where((cnt < SEL_TOPK) & (imp > NEG / 2), 1.0, 0.0)
    msel_ref[...] = _dot(sel.astype(BF16), exp_ref[...])

    def blk(kref, vref, kb):
        s0 = pl.multiple_of(kb * tq, tq)
        return kref[pl.ds(s0, tq), :], vref[pl.ds(s0, tq), :]

    def attend(carry, kb, k, v, ok3):
        s3 = (_dot_nt(Q, k) * SCALE128).reshape(H, tq, tq)
        return _online(carry, jnp.where(ok3, s3, NEG).reshape(R, tq), v)

    def sel_body(kb, carry):
        k, v = blk(sk_ref, sv_ref, kb)
        on = msel_ref[:, pl.ds(pl.multiple_of(kb * tq, tq), tq)] > 0.5
        keypos = kb * tq + lane3
        return attend(carry, kb, k, v, on[None] & (keypos <= qp3))

    m, l, acc = lax.fori_loop(0, i + 1, sel_body, _init_carry(R, LANES))
    o_s = acc / jnp.maximum(l, TINY)

    def win_body(kb, carry):
        k, v = blk(wk_ref, wv_ref, kb)
        dist = qp3 - (kb * tq + lane3)
        return attend(carry, kb, k, v, (dist >= 0) & (dist <= WINDOW))

    m, l, acc = lax.fori_loop(jnp.maximum(i - WINDOW // tq, 0), i + 1, win_body, _init_carry(R, LANES))
    o_w = acc / jnp.maximum(l, TINY)

    gates = g_ref[...]
    for r in range(H):
        rs = slice(r * tq, (r + 1) * tq)
        o = (gates[:, r:r + 1] * o_c[rs] + gates[:, H + r:H + r + 1] * o_s[rs]
             + gates[:, 2 * H + r:2 * H + r + 1] * o_w[rs])
        o_ref[:, r * LANES:(r + 1) * LANES] = o.astype(o_ref.dtype)


def _importance_matrix(n_tok_pad, n_tok, n_blk_pad, n_blk):
    r_ = SEL_BLOCK // CMP_STRIDE
    f_ = CMP_BLOCK // CMP_STRIDE - 1
    i = np.arange(n_tok_pad)[:, None]
    j = np.arange(n_blk_pad)[None, :]
    a = (i >= r_ * j - f_) & (i <= r_ * j + r_ - 1) & (i < n_tok) & (j < n_blk)
    return jnp.asarray(a.astype(np.float32))


def _expand_matrix(n_blk_pad, n_keys):
    j = np.arange(n_blk_pad)[:, None]
    t = np.arange(n_keys)[None, :]
    return jnp.asarray((t // SEL_BLOCK == j).astype(np.float32), dtype=BF16)


def nsa_prompt(nq, ng, cmp_tok, rows_b, win_b, *, batch, seq):
    M = nq.shape[0]
    tq = 128
    nt = seq // tq
    G = NSA_KV_HEADS
    n_tok = seq // CMP_STRIDE
    assert n_tok == LANES and seq // SEL_BLOCK <= LANES
    imp_m = _importance_matrix(LANES, n_tok - 1, LANES, seq // SEL_BLOCK)
    exp_m = _expand_matrix(LANES, seq)
    kv = lambda c: pl.BlockSpec((seq, LANES), lambda b, g, i: (b, c + g))
    return pl.pallas_call(
        functools.partial(_nsa_prompt_kernel, tq=tq, seq=seq), grid=(batch, G, nt),
        in_specs=[pl.BlockSpec((tq, NSA_REP * LANES), lambda b, g, i: (b * nt + i, g)),
                  pl.BlockSpec((tq, LANES), lambda b, g, i: (b * nt + i, g)),
                  pl.BlockSpec((1, 1, n_tok, LANES), lambda b, g, i: (b, g, 0, 0)),
                  pl.BlockSpec((1, 1, n_tok, LANES), lambda b, g, i: (b, G + g, 0, 0)),
                  kv(4), kv(6), kv(0), kv(2),
                  pl.BlockSpec((LANES, LANES), lambda b, g, i: (0, 0)),
                  pl.BlockSpec((LANES, seq), lambda b, g, i: (0, 0))],
        out_specs=pl.BlockSpec((tq, NSA_REP * LANES), lambda b, g, i: (b * nt + i, g)),
        out_shape=jax.ShapeDtypeStruct((M, NSA_HEADS * LANES), BF16),
        scratch_shapes=[pltpu.VMEM((tq, seq), F32)],
        compiler_params=_cparams("parallel", "parallel", "arbitrary"), name="nsa_prompt",
    )(nq, ng, cmp_tok, cmp_tok, rows_b, rows_b, win_b, win_b, imp_m, exp_m)


def _diff_sample_kernel(pt_ref, lam_ref, gn_ref, qm_ref, kvn_ref, *refs, P, lam_init):
    pages = refs[:P]
    o_ref = refs[P]
    m_ref, l_ref, acc_ref = refs[P + 1:]
    j = pl.program_id(1)
    W = DIFF_KV_HEADS * LANES

    @pl.when(j == 0)
    def _():
        m_ref[...] = jnp.full(m_ref.shape, NEG, F32)
        l_ref[...] = jnp.zeros(l_ref.shape, F32)
        acc_ref[...] = jnp.zeros(acc_ref.shape, F32)

    Qm = qm_ref[0]
    s = jnp.concatenate([_dot_nt(Qm, pages[p][0, 0, :, 0:W].astype(BF16)) for p in range(P)], axis=1)
    v = jnp.concatenate([pages[p][0, 0, :, W:2 * W].astype(BF16) for p in range(P)], axis=0)
    m, l, acc = _online((m_ref[...], l_ref[...], acc_ref[...]), s, v)
    m_ref[...], l_ref[...], acc_ref[...] = m, l, acc

    @pl.when(j == pl.num_programs(1) - 1)
    def _():
        kn = kvn_ref[0, :, 0:W].astype(F32)
        vn = kvn_ref[0, :, W:2 * W].astype(F32)
        s_new = jnp.sum(Qm.astype(F32) * kn, axis=-1, keepdims=True)
        m2 = jnp.maximum(m, s_new)
        a2 = jnp.exp(m - m2)
        pn = jnp.exp(s_new - m2)
        a = (a2 * acc + pn * vn) / jnp.maximum(a2 * l + pn, TINY)
        lam = _lambda(lam_ref, lam_init)
        g = gn_ref[...]
        outs = []
        for grp in range(DIFF_KV_HEADS):
            for r in range(2):
                r0 = grp * 4 + r * 2
                cols = slice(grp * LANES, (grp + 1) * LANES)
                o = a[r0:r0 + 1, cols] - lam * a[r0 + 1:r0 + 2, cols]
                outs.append(_head_norm(o, g, 1.0 - lam_init))
        o_ref[0] = jnp.concatenate(outs, axis=1).astype(o_ref.dtype)


def diff_sample(qm, kv_new, cache, page_table_flat, lam_param, out_norm, lam_init, *, layer, batch, n_pages):
    P = 8
    W2 = 2 * DIFF_KV_HEADS * LANES
    page = lambda p: pl.BlockSpec((1, 1, PAGE_SIZE, W2),
                                  lambda b, j, pt: (layer, pt[b * n_pages + j * P + p], 0, 0))
    gs = pltpu.PrefetchScalarGridSpec(
        num_scalar_prefetch=1, grid=(batch, n_pages // P),
        in_specs=[pl.BlockSpec((4, DIFF_DH), lambda b, j, pt: (0, 0)),
                  pl.BlockSpec((1, LANES), lambda b, j, pt: (0, 0)),
                  pl.BlockSpec((1, 16, DIFF_KV_HEADS * LANES), lambda b, j, pt: (b, 0, 0)),
                  pl.BlockSpec((1, 1, W2), lambda b, j, pt: (b, 0, 0))] + [page(p) for p in range(P)],
        out_specs=pl.BlockSpec((1, 1, DIFF_HEADS * LANES), lambda b, j, pt: (b, 0, 0)),
        scratch_shapes=[pltpu.VMEM((16, 1), F32), pltpu.VMEM((16, 1), F32),
                        pltpu.VMEM((16, DIFF_KV_HEADS * LANES), F32)])
    return pl.pallas_call(
        functools.partial(_diff_sample_kernel, P=P, lam_init=lam_init), grid_spec=gs,
        out_shape=jax.ShapeDtypeStruct((batch, 1, DIFF_HEADS * LANES), BF16),
        compiler_params=_cparams("parallel", "arbitrary"), name="diff_sample",
    )(page_table_flat, lam_param, out_norm.reshape(1, LANES), qm, kv_new, *([cache] * P))


def _moba_gate_kernel(pt_ref, qm_ref, p0_ref, p1_ref, o_ref, gate_ref, *, nblk):
    n = pl.program_id(1)
    lane = lax.broadcasted_iota(I32, (MOBA_HEADS, LANES), 1)

    @pl.when(n == 0)
    def _():
        gate_ref[...] = jnp.full(gate_ref.shape, NEG, F32)

    ksum = jnp.sum(p0_ref[0, 0], axis=0, keepdims=True) + jnp.sum(p1_ref[0, 0], axis=0, keepdims=True)
    kmean = (ksum * (1.0 / MOBA_BLOCK)).astype(BF16).astype(F32)
    g = jnp.sum(qm_ref[0].astype(F32) * kmean, axis=-1, keepdims=True)
    gate_ref[...] = jnp.where(lane == n, g, gate_ref[...])

    @pl.when(n == nblk - 1)
    def _():
        gate = gate_ref[...]
        cnt = _rank_lt(gate, nblk, lane)
        lanef = lane.astype(F32)
        out = jnp.zeros(gate.shape, F32)
        for k in range(MOBA_TOPK):
            hit = cnt == float(k)
            idx = jnp.sum(jnp.where(hit, lanef, 0.0), axis=-1, keepdims=True)
            val = jnp.max(jnp.where(hit, gate, NEG), axis=-1, keepdims=True)
            out = out + jnp.where(lane == k, idx, 0.0) + jnp.where((lane == 4 + k) & (val > NEG / 2), 1.0, 0.0)
        o_ref[0] = out.astype(I32)


def moba_gate_sample(qm, cache, page_table_flat, *, layer, batch, n_pages):
    nblk = n_pages * PAGE_SIZE // MOBA_BLOCK
    W = MOBA_KV_HEADS * LANES
    ppb = MOBA_BLOCK // PAGE_SIZE
    assert ppb == 2 and nblk <= LANES
    page = lambda p: pl.BlockSpec((1, 1, PAGE_SIZE, W), lambda b, n, pt: (layer, pt[b * n_pages + n * ppb + p], 0, 0))
    gs = pltpu.PrefetchScalarGridSpec(
        num_scalar_prefetch=1, grid=(batch, nblk),
        in_specs=[pl.BlockSpec((1, MOBA_HEADS, W), lambda b, n, pt: (b, 0, 0)), page(0), page(1)],
        out_specs=pl.BlockSpec((1, MOBA_HEADS, LANES), lambda b, n, pt: (b, 0, 0)),
        scratch_shapes=[pltpu.VMEM((MOBA_HEADS, LANES), F32)])
    return pl.pallas_call(
        functools.partial(_moba_gate_kernel, nblk=nblk), grid_spec=gs,
        out_shape=jax.ShapeDtypeStruct((batch, MOBA_HEADS, LANES), I32),
        compiler_params=_cparams("parallel", "arbitrary"), name="moba_gate_sample",
    )(page_table_flat, qm, cache, cache)


def _moba_sample_kernel(pg_ref, ok_ref, q_ref, kn_ref, vn_ref, k_ref, v_ref, o_ref, m_ref, l_ref, acc_ref, *, nstep):
    b, h, t = pl.program_id(0), pl.program_id(1), pl.program_id(2)

    @pl.when(t == 0)
    def _():
        m_ref[...] = jnp.full(m_ref.shape, NEG, F32)
        l_ref[...] = jnp.zeros(l_ref.shape, F32)
        acc_ref[...] = jnp.zeros(acc_ref.shape, F32)

    Q = q_ref[0, 0]
    on = ok_ref[(b * MOBA_HEADS + h) * MOBA_TOPK + t // 2] > 0
    s = jnp.where(on, _dot_nt(Q, k_ref[0, 0].astype(BF16)) * SCALE128, NEG)
    okm = jnp.broadcast_to(on, s.shape)
    m, l, acc = _online((m_ref[...], l_ref[...], acc_ref[...]), s, v_ref[0, 0].astype(BF16), ok=okm)
    m_ref[...], l_ref[...], acc_ref[...] = m, l, acc

    @pl.when(t == nstep - 1)
    def _():
        kn = kn_ref[0].astype(F32)
        vn = vn_ref[0].astype(F32)
        s_new = jnp.sum(Q.astype(F32) * kn, axis=-1, keepdims=True) * SCALE128
        m2 = jnp.maximum(m, s_new)
        a2 = jnp.exp(m - m2)
        pn = jnp.exp(s_new - m2)
        o = (a2 * acc + pn * vn) / jnp.maximum(a2 * l + pn, TINY)
        o_ref[0] = o[0:1].astype(o_ref.dtype)


def moba_sample(q8, kv_new, cache, sel_pages, sel_ok, *, layer, batch):
    nstep = MOBA_TOPK * (MOBA_BLOCK // PAGE_SIZE)
    G = MOBA_KV_HEADS
    pg = lambda c: pl.BlockSpec(
        (1, 1, PAGE_SIZE, LANES),
        lambda b, h, t, pgs, oks: (layer, pgs[(b * MOBA_HEADS + h) * nstep + t], 0, c + h // 2))
    gs = pltpu.PrefetchScalarGridSpec(
        num_scalar_prefetch=2, grid=(batch, MOBA_HEADS, nstep),
        in_specs=[pl.BlockSpec((1, 1, 8, LANES), lambda b, h, t, pgs, oks: (b, h, 0, 0)),
                  pl.BlockSpec((1, 1, LANES), lambda b, h, t, pgs, oks: (b, 0, h // 2)),
                  pl.BlockSpec((1, 1, LANES), lambda b, h, t, pgs, oks: (b, 0, G + h // 2)),
                  pg(0), pg(G)],
        out_specs=pl.BlockSpec((1, 1, LANES), lambda b, h, t, pgs, oks: (b, 0, h)),
        scratch_shapes=[pltpu.VMEM((8, 1), F32), pltpu.VMEM((8, 1), F32), pltpu.VMEM((8, LANES), F32)])
    return pl.pallas_call(
        functools.partial(_moba_sample_kernel, nstep=nstep), grid_spec=gs,
        out_shape=jax.ShapeDtypeStruct((batch, 1, MOBA_HEADS * LANES), BF16),
        compiler_params=_cparams("parallel", "parallel", "arbitrary"), name="moba_sample",
    )(sel_pages, sel_ok, q8, kv_new, kv_new, cache, cache)


def _nsa_chunk_kernel(pt_ref, p_ref, o_ref, slab_ref):
    n = PAGE_SIZE // CMP_STRIDE
    for cg in range(4):
        slab_ref[...] = p_ref[0, 0, :, cg * LANES:(cg + 1) * LANES]
        for j in range(CMP_STRIDE):
            o_ref[0, cg, :, j * LANES:(j + 1) * LANES] = slab_ref[pl.ds(j, n, stride=CMP_STRIDE), :]


def nsa_chunk_sample(cache, page_table_flat, *, layer, batch, n_pages):
    n = PAGE_SIZE // CMP_STRIDE
    gs = pltpu.PrefetchScalarGridSpec(
        num_scalar_prefetch=1, grid=(batch, n_pages),
        in_specs=[pl.BlockSpec((1, 1, PAGE_SIZE, 4 * LANES), lambda b, p, pt: (layer, pt[b * n_pages + p], 0, 0))],
        out_specs=pl.BlockSpec((1, 4, n, CMP_STRIDE * LANES), lambda b, p, pt: (b, 0, p, 0)),
        scratch_shapes=[pltpu.VMEM((PAGE_SIZE, LANES), F32)])
    return pl.pallas_call(
        _nsa_chunk_kernel, grid_spec=gs,
        out_shape=jax.ShapeDtypeStruct((batch, 4, n_pages * n, CMP_STRIDE * LANES), F32),
        compiler_params=_cparams("parallel", "parallel"), name="nsa_chunk_sample",
    )(page_table_flat, cache)


def _nsa_cmp_sample_kernel(q_ref, ck_ref, cv_ref, imp_ref, oc_ref, sel_ref, *, n_tok, n_valid, cur, n_blk_pad):
    Q = q_ref[0, 0]
    lane = lax.broadcasted_iota(I32, (NSA_REP, n_tok), 1)
    ok = lane < n_valid
    s = jnp.where(ok, _dot_nt(Q, ck_ref[0, 0]) * SCALE128, NEG)
    p = jnp.where(ok, jnp.exp(s - jnp.max(s, axis=-1, keepdims=True)), 0.0)
    pc = p / jnp.maximum(jnp.sum(p, axis=-1, keepdims=True), TINY)
    oc_ref[0, 0] = _dot(pc.astype(BF16), cv_ref[0, 0])
    pcs = jnp.broadcast_to(jnp.sum(pc, axis=0, keepdims=True), (8, n_tok))
    imp = _dot_exact(pcs, imp_ref[...])[0:1]
    blk = lax.broadcasted_iota(I32, (1, n_blk_pad), 1)
    valid = blk <= cur
    forced = (blk == 0) | (blk == cur) | (blk == cur - 1)
    imp = jnp.where(valid, jnp.where(forced, FORCE_SCORE, imp), NEG)
    rows = jnp.broadcast_to(imp, (n_blk_pad, n_blk_pad))
    cols = rows.T
    mi = lax.broadcasted_iota(I32, (n_blk_pad, n_blk_pad), 0)
    ji = lax.broadcasted_iota(I32, (n_blk_pad, n_blk_pad), 1)
    beats = (cols > rows) | ((cols == rows) & (mi < ji))
    cnt = jnp.sum(jnp.where(beats, 1.0, 0.0), axis=0, keepdims=True)
    blkf = blk.astype(F32)
    lane_o = lax.broadcasted_iota(I32, (1, LANES), 1)
    out = jnp.zeros((1, LANES), F32)
    for k in range(SEL_TOPK):
        hit = cnt == float(k)
        idx = jnp.sum(jnp.where(hit, blkf, 0.0), axis=-1, keepdims=True)
        val = jnp.max(jnp.where(hit, imp, NEG), axis=-1, keepdims=True)
        out = out + jnp.where(lane_o == k, idx, 0.0) + jnp.where((lane_o == SEL_TOPK + k) & (val > NEG / 2), 1.0, 0.0)
    sel_ref[0, 0] = jnp.broadcast_to(out, (8, LANES)).astype(I32)


def nsa_cmp_sample(q8, cmp_tok, *, batch, past_len):
    G = NSA_KV_HEADS
    n_tok = cmp_tok.shape[2]
    n_valid = n_tok - 1
    cur = past_len // SEL_BLOCK
    n_blk = cur + 1
    n_blk_pad = -(-n_blk // LANES) * LANES
    imp_m = _importance_matrix(n_tok, n_tok, n_blk_pad, n_blk)
    return pl.pallas_call(
        functools.partial(_nsa_cmp_sample_kernel, n_tok=n_tok, n_valid=n_valid, cur=cur, n_blk_pad=n_blk_pad),
        grid=(batch, G),
        in_specs=[pl.BlockSpec((1, 1, NSA_REP, LANES), lambda b, g: (b, g, 0, 0)),
                  pl.BlockSpec((1, 1, n_tok, LANES), lambda b, g: (b, g, 0, 0)),
                  pl.BlockSpec((1, 1, n_tok, LANES), lambda b, g: (b, G + g, 0, 0)),
                  pl.BlockSpec((n_tok, n_blk_pad), lambda b, g: (0, 0))],
        out_specs=[pl.BlockSpec((1, 1, NSA_REP, LANES), lambda b, g: (b, g, 0, 0)),
                   pl.BlockSpec((1, 1, 8, LANES), lambda b, g: (b, g, 0, 0))],
        out_shape=[jax.ShapeDtypeStruct((batch, G, NSA_REP, LANES), F32),
                   jax.ShapeDtypeStruct((batch, G, 8, LANES), I32)],
        compiler_params=_cparams("parallel", "parallel"), name="nsa_cmp_sample",
    )(q8, cmp_tok, cmp_tok, imp_m)


def _nsa_sel_sample_kernel(pg_ref, hb_ref, ok_ref, own_ref, q_ref, gt_ref, oc_ref, kn_ref, vn_ref, wkn_ref,
                           wvn_ref, wk_ref, wv_ref, k_ref, v_ref, o_ref, m_ref, l_ref, acc_ref):
    b, g, t = pl.program_id(0), pl.program_id(1), pl.program_id(2)

    @pl.when(t == 0)
    def _():
        m_ref[...] = jnp.full(m_ref.shape, NEG, F32)
        l_ref[...] = jnp.zeros(l_ref.shape, F32)
        acc_ref[...] = jnp.zeros(acc_ref.shape, F32)

    Q = q_ref[0, 0]
    on = ok_ref[(b * NSA_KV_HEADS + g) * SEL_TOPK + t] > 0
    s = jnp.where(on, _dot_nt(Q, k_ref[0, 0].astype(BF16)) * SCALE128, NEG)
    okm = jnp.broadcast_to(on, s.shape)
    m, l, acc = _online((m_ref[...], l_ref[...], acc_ref[...]), s, v_ref[0, 0].astype(BF16), ok=okm)
    m_ref[...], l_ref[...], acc_ref[...] = m, l, acc

    @pl.when(t == SEL_TOPK - 1)
    def _():
        Qf = Q.astype(F32)

        def with_new(m, l, acc, kn, vn, on):
            s_new = jnp.where(on, jnp.sum(Qf * kn, axis=-1, keepdims=True) * SCALE128, NEG)
            m2 = jnp.maximum(m, s_new)
            a2 = jnp.exp(m - m2)
            pn = jnp.where(on, jnp.exp(s_new - m2), 0.0)
            return (a2 * acc + pn * vn) / jnp.maximum(a2 * l + pn, TINY)

        own = own_ref[b * NSA_KV_HEADS + g] > 0
        o_s = with_new(m, l, acc, kn_ref[0].astype(F32), vn_ref[0].astype(F32), own)
        sw = _dot_nt(Q, wk_ref[0, 0].astype(BF16)) * SCALE128
        mw, lw, aw = _online(_init_carry(NSA_REP, LANES), sw, wv_ref[0, 0].astype(BF16))
        o_w = with_new(mw, lw, aw, wkn_ref[0].astype(F32), wvn_ref[0].astype(F32), True)
        gt = gt_ref[0, 0]
        o_ref[0, 0] = gt[:, 0:1] * oc_ref[0, 0] + gt[:, 1:2] * o_s + gt[:, 2:3] * o_w


def nsa_sel_sample(q8, gates8, o_c, rows_new, win_new, state_win, cache, sel_pages, sel_half, sel_ok, sel_own, *,
                   layer, batch):
    G = NSA_KV_HEADS
    wlen = state_win.shape[2]
    idx = lambda b, g, t: (b * G + g) * SEL_TOPK + t
    blk = lambda c: pl.BlockSpec(
        (1, 1, SEL_BLOCK, LANES),
        lambda b, g, t, pgs, hbs, oks, own: (layer, pgs[idx(b, g, t)], hbs[idx(b, g, t)], c + g))
    new = lambda c: pl.BlockSpec((1, 1, LANES), lambda b, g, t, pgs, hbs, oks, own: (b, 0, c + g))
    per_bg = pl.BlockSpec((1, 1, NSA_REP, LANES), lambda b, g, t, pgs, hbs, oks, own: (b, g, 0, 0))
    win = lambda c: pl.BlockSpec((1, 1, wlen, LANES), lambda b, g, t, pgs, hbs, oks, own: (layer, b, 0, c + g))
    gs = pltpu.PrefetchScalarGridSpec(
        num_scalar_prefetch=4, grid=(batch, G, SEL_TOPK),
        in_specs=[per_bg, per_bg, per_bg, new(4), new(6), new(0), new(2), win(0), win(G), blk(4), blk(6)],
        out_specs=per_bg,
        scratch_shapes=[pltpu.VMEM((8, 1), F32), pltpu.VMEM((8, 1), F32), pltpu.VMEM((8, LANES), F32)])
    return pl.pallas_call(
        _nsa_sel_sample_kernel, grid_spec=gs,
        out_shape=jax.ShapeDtypeStruct((batch, G, NSA_REP, LANES), F32),
        compiler_params=_cparams("parallel", "parallel", "arbitrary"), name="nsa_sel_sample",
    )(sel_pages, sel_half, sel_ok, sel_own, q8, gates8, o_c, rows_new, rows_new, win_new, win_new,
      state_win, state_win, cache, cache)


def _prep_layer_weights(l, ffn1_w_gu, ffn1_w_down, w_in, w_branch, w_out, ffn2_w_gu, ffn2_w_down,
                        nsa_cmp_w1, nsa_cmp_w2, nsa_cmp_pos):
    bf = lambda w: w.astype(BF16)
    w = w_in[l]
    src = np.zeros((NSA_KV_HEADS, 3, NSA_REP), np.int32)
    for g in range(NSA_KV_HEADS):
        for t in range(3):
            for r in range(NSA_REP):
                src[g, t, r] = (g * NSA_REP + r) * 3 + t
    w_ng = w[:, QKV_COLS:QKV_COLS + NG_COLS][:, src.reshape(NSA_KV_HEADS, 3 * NSA_REP)]
    w_ng = jnp.pad(w_ng, ((0, 0), (0, 0), (0, LANES - 3 * NSA_REP))).reshape(D_MODEL, NG_PAD)
    W = CMP_STRIDE * HEAD_DIM
    return dict(
        gu1=bf(ffn1_w_gu[l]), down1=bf(ffn1_w_down[l]), gu2=bf(ffn2_w_gu[l]), down2=bf(ffn2_w_down[l]),
        qkv=bf(w[:, :QKV_COLS]), ng=bf(w_ng), mg=bf(w[:, QKV_COLS + NG_COLS:]),
        branch=bf(w_branch[l]), out=bf(w_out[l]),
        cw0=bf(nsa_cmp_w1[l][:, :CMP_STRIDE].reshape(2, W, CMP_HIDDEN)),
        cw1=bf(nsa_cmp_w1[l][:, CMP_STRIDE:].reshape(2, W, CMP_HIDDEN)),
        cw2=bf(nsa_cmp_w2[l]),
        cpe=nsa_cmp_pos[l].reshape(2, 2, W),
    )


def _ffn(x, norm_g, w_gu, w_down):
    h = rmsnorm_bf16(x, norm_g)
    act = matmul_swiglu(h, w_gu)
    return matmul_residual(act, w_down, x, 0.5, tm=512, tn=256, name="ffn_down")


def _project(x, mix_g, wl, tables, gd, gm, gn, *, seq, batch):
    h = rmsnorm_bf16(x, mix_g)
    proj = matmul(h, wl['qkv'], tn=512, name="in_proj_qkv")
    ng = matmul(h, wl['ng'], tn=NG_PAD, sigmoid=True, name="in_proj_gate")
    mg = matmul(h, wl['mg'], tn=512, sigmoid=True, name="in_proj_merge_gate")
    post = post_project(proj, tables, gd, gm, gn, seq=seq, batch=batch)
    return post, ng, mg


def _merge(x, o_d, o_m, o_n, mg, wl):
    u = matmul_merge(o_d, o_m, o_n, wl['branch'], mg)
    return matmul_residual(u, wl['out'], x, 1.0, tm=1024, tn=512, name="out_proj")


def kernel(x_prompt, x_sample, cache_diff, cache_moba, cache_nsa, state_nsa_win, page_table, ffn1_norm, ffn1_w_gu, ffn1_w_down, mix_norm, w_in, diff_qk_norm, diff_lambda, diff_out_norm, moba_qk_norm, nsa_qk_norm, nsa_cmp_w1, nsa_cmp_w2, nsa_cmp_pos, w_branch, w_out, ffn2_norm, ffn2_w_gu, ffn2_w_down):
    B, T, D = x_prompt.shape
    Bs = x_sample.shape[0]
    depth = ffn1_norm.shape[0]
    n_pool = cache_diff.shape[1]
    n_pages = page_table.shape[1]
    past_len = n_pages * PAGE_SIZE
    MS = 16
    assert x_sample.shape[1] == 1 and Bs <= MS and T % MOBA_BLOCK == 0

    xp = x_prompt.reshape(B * T, D)
    xs = jnp.pad(x_sample.reshape(Bs, D), ((0, MS - Bs), (0, 0)))
    c_diff = cache_diff.reshape(depth, n_pool, PAGE_SIZE, 2 * DIFF_KV_HEADS * HEAD_DIM)
    c_moba = cache_moba.reshape(depth, n_pool, PAGE_SIZE, 2 * MOBA_KV_HEADS * HEAD_DIM)
    c_nsa = cache_nsa.reshape(depth, n_pool, PAGE_SIZE, 4 * NSA_KV_HEADS * HEAD_DIM)
    wlen = state_nsa_win.shape[2]
    s_win = state_nsa_win.reshape(depth, Bs, wlen, 2 * NSA_KV_HEADS * HEAD_DIM)
    pt_flat = page_table.reshape(-1).astype(I32)
    tab_p = _rope_tables(jnp.arange(T, dtype=I32))
    tab_s = _rope_tables(jnp.full((MS,), past_len, I32))

    outs = [[] for _ in range(8)]
    for l in range(depth):
        lam_init = 0.8 - 0.6 * math.exp(-0.3 * l)
        wl = _prep_layer_weights(l, ffn1_w_gu, ffn1_w_down, w_in, w_branch, w_out, ffn2_w_gu, ffn2_w_down,
                                 nsa_cmp_w1, nsa_cmp_w2, nsa_cmp_pos)
        gd = jnp.tile(diff_qk_norm[l], (1, 2))
        gm, gn = moba_qk_norm[l], nsa_qk_norm[l]
        lam_p = diff_lambda[l]

        xp = _ffn(xp, ffn1_norm[l], wl['gu1'], wl['down1'])
        xs = _ffn(xs, ffn1_norm[l], wl['gu1'], wl['down1'])

        post, ng, mg = _project(xp, mix_norm[l], wl, tab_p, gd, gm, gn, seq=T, batch=B)
        dq, rd, rdb, mq, rm, rmb, nq, rn, rnb, win, winb, xc = post
        o_d = diff_prompt(dq, rdb, lam_p, diff_out_norm[l], lam_init, batch=B, seq=T)
        o_m = moba_prompt(mq, rm, rmb, batch=B, seq=T)
        cmp_tok = compress(xc, wl['cpe'], wl['cw0'], wl['cw1'], wl['cw2'])
        o_n = nsa_prompt(nq, ng, cmp_tok, rnb, winb, batch=B, seq=T)
        xp = _merge(xp, o_d, o_m, o_n, mg, wl)
        outs[0].append(rd.reshape(B, T, 2, DIFF_KV_HEADS, HEAD_DIM))
        outs[2].append(rm.reshape(B, T, 2, MOBA_KV_HEADS, HEAD_DIM))
        outs[4].append(rn.reshape(B, T, 4, NSA_KV_HEADS, HEAD_DIM))
        wkeep = min(WINDOW, T)
        outs[6].append(win.reshape(B, T, 2, NSA_KV_HEADS, HEAD_DIM)[:, T - wkeep:])

        post, ng, mg = _project(xs, mix_norm[l], wl, tab_s, gd, gm, gn, seq=1, batch=Bs)
        dq, rd, rdb, mq, rm, rmb, nq, rn, rnb, win, winb = post
        q5 = dq[:Bs].reshape(Bs, DIFF_KV_HEADS, 2, 2, DIFF_DH)
        eye_g = jnp.eye(DIFF_KV_HEADS, dtype=BF16)
        eye_c = jnp.eye(2, dtype=BF16)
        qm = (q5[:, :, :, :, None, None, :] * eye_g[None, :, None, None, :, None, None]
              * eye_c[None, None, None, :, None, :, None]).reshape(Bs, 16, DIFF_KV_HEADS * LANES)
        o_d = diff_sample(qm, rdb[:Bs].reshape(Bs, 1, 1024), c_diff, pt_flat, lam_p, diff_out_norm[l], lam_init,
                          layer=l, batch=Bs, n_pages=n_pages)
        mq4 = mq[:Bs].reshape(Bs, MOBA_KV_HEADS, 2, HEAD_DIM)
        eye_m = jnp.eye(MOBA_KV_HEADS, dtype=BF16)
        qm2 = (mq4[:, :, :, None, :] * eye_m[None, :, None, :, None]).reshape(Bs, MOBA_HEADS, MOBA_KV_HEADS * LANES)
        msel = moba_gate_sample(qm2, c_moba, pt_flat, layer=l, batch=Bs, n_pages=n_pages)
        m_idx = msel[:, :, 0:MOBA_TOPK]
        m_ok = msel[:, :, 4:4 + MOBA_TOPK]
        ppb = MOBA_BLOCK // PAGE_SIZE
        logical = (m_idx[..., None] * ppb + jnp.arange(ppb, dtype=I32)).reshape(Bs, MOBA_HEADS * MOBA_TOPK * ppb)
        m_pages = jnp.take_along_axis(page_table.astype(I32), logical, axis=1)
        q8 = jnp.broadcast_to(mq[:Bs].reshape(Bs, MOBA_HEADS, 1, HEAD_DIM), (Bs, MOBA_HEADS, 8, HEAD_DIM))
        o_m = moba_sample(q8, rmb[:Bs].reshape(Bs, 1, 1024), c_moba, m_pages.reshape(-1), m_ok.reshape(-1),
                          layer=l, batch=Bs)
        xcs = nsa_chunk_sample(c_nsa, pt_flat, layer=l, batch=Bs, n_pages=n_pages)
        cmp_tok = compress(xcs, wl['cpe'], wl['cw0'], wl['cw1'], wl['cw2'])
        nq8 = nq[:Bs].reshape(Bs, NSA_KV_HEADS, NSA_REP, HEAD_DIM)
        o_c, nsel = nsa_cmp_sample(nq8, cmp_tok, batch=Bs, past_len=past_len)
        n_idx = nsel[:, :, 0, 0:SEL_TOPK]
        n_ok = nsel[:, :, 0, SEL_TOPK:2 * SEL_TOPK]
        bpp = PAGE_SIZE // SEL_BLOCK
        in_past = n_idx < past_len // SEL_BLOCK
        n_log = jnp.where(in_past, n_idx // bpp, 0)
        n_pages_sel = jnp.take_along_axis(page_table.astype(I32), n_log.reshape(Bs, -1), axis=1)
        n_half = jnp.where(in_past, n_idx % bpp, 0)
        n_on = (n_ok > 0) & in_past
        n_own = jnp.any((n_ok > 0) & (n_idx == past_len // SEL_BLOCK), axis=-1)
        gates8 = jnp.pad(ng[:Bs].reshape(Bs, NSA_KV_HEADS, LANES)[:, :, :3 * NSA_REP]
                         .reshape(Bs, NSA_KV_HEADS, 3, NSA_REP).transpose(0, 1, 3, 2),
                         ((0, 0), (0, 0), (0, 0), (0, LANES - 3)))
        o_n = nsa_sel_sample(nq8, gates8, o_c, rnb[:Bs].reshape(Bs, 1, 1024), winb[:Bs].reshape(Bs, 1, 512),
                             s_win, c_nsa, n_pages_sel.reshape(-1), n_half.reshape(-1).astype(I32),
                             n_on.reshape(-1).astype(I32), n_own.reshape(-1).astype(I32), layer=l, batch=Bs)
        pad_rows = lambda a, w: jnp.pad(a.reshape(Bs, w).astype(BF16), ((0, MS - Bs), (0, 0)))
        xs = _merge(xs, pad_rows(o_d, 1024), pad_rows(o_m, 1024), pad_rows(o_n, 2048), mg, wl)
        outs[1].append(rd[:Bs].reshape(Bs, 1, 2, DIFF_KV_HEADS, HEAD_DIM))
        outs[3].append(rm[:Bs].reshape(Bs, 1, 2, MOBA_KV_HEADS, HEAD_DIM))
        outs[5].append(rn[:Bs].reshape(Bs, 1, 4, NSA_KV_HEADS, HEAD_DIM))
        new_win = win[:Bs].reshape(Bs, 1, 2, NSA_KV_HEADS, HEAD_DIM)
        outs[7].append(jnp.concatenate([state_nsa_win[l], new_win], axis=1)[:, 1:])

        xp = _ffn(xp, ffn2_norm[l], wl['gu2'], wl['down2'])
        xs = _ffn(xs, ffn2_norm[l], wl['gu2'], wl['down2'])

    return (xp.reshape(B, T, D), xs[:Bs].reshape(Bs, 1, D)) + tuple(jnp.stack(o) for o in outs)
```

```python
import functools
import math

import jax
import jax.numpy as jnp
import numpy as np
from jax import lax
from jax.experimental import pallas as pl
from jax.experimental.pallas import tpu as pltpu

F32 = jnp.float32
BF16 = jnp.bfloat16
I32 = jnp.int32

D_MODEL = 4096
D_FF = 11008
HEAD_DIM = 128
PAGE_SIZE = 128
ROPE_THETA = 10000.0
EPS = 1e-6
NEG = -1e30
TINY = 1e-30
DIFF_HEADS, DIFF_KV_HEADS, DIFF_DH = 8, 4, 64
MOBA_HEADS, MOBA_KV_HEADS, MOBA_BLOCK, MOBA_TOPK = 8, 4, 256, 3
NSA_HEADS, NSA_KV_HEADS = 16, 2
NSA_REP = NSA_HEADS // NSA_KV_HEADS
CMP_BLOCK, CMP_STRIDE, CMP_HIDDEN = 32, 16, 256
SEL_BLOCK, SEL_TOPK, WINDOW = 64, 16, 512
FORCE_SCORE = 1e4
QKV_COLS = 7680
NG_COLS = NSA_HEADS * 3
NG_PAD = NSA_KV_HEADS * 128
MG_COLS = 3 * D_MODEL

LANES = 128
SUBLANES = 8
VMEM_LIMIT_BYTES = 56 * 1024 * 1024
SCALE128 = HEAD_DIM ** -0.5
SLOTS = 8


def _cparams(*sem):
    return pltpu.CompilerParams(dimension_semantics=sem, vmem_limit_bytes=VMEM_LIMIT_BYTES)


def _dot(a, b):
    return jnp.dot(a, b, preferred_element_type=F32)


def _dot_nt(a, b):
    return lax.dot_general(a, b, (((1,), (1,)), ((), ())), preferred_element_type=F32)


def _dot_exact(a, b):
    return jnp.dot(a, b, preferred_element_type=F32, precision=lax.Precision.HIGHEST)


def _rmsnorm_kernel(x_ref, g_ref, o_ref):
    x = x_ref[...]
    ms = jnp.mean(x * x, axis=-1, keepdims=True)
    o_ref[...] = (x * lax.rsqrt(ms + EPS) * g_ref[...]).astype(o_ref.dtype)


def rmsnorm_bf16(x, g):
    M, D = x.shape
    tm = min(M, 256)
    return pl.pallas_call(
        _rmsnorm_kernel, grid=(M // tm,),
        in_specs=[pl.BlockSpec((tm, D), lambda i: (i, 0)), pl.BlockSpec((1, D), lambda i: (0, 0))],
        out_specs=pl.BlockSpec((tm, D), lambda i: (i, 0)),
        out_shape=jax.ShapeDtypeStruct((M, D), BF16),
        compiler_params=_cparams("parallel"), name="rmsnorm",
    )(x, g.reshape(1, D))


def _mm_kernel(a_ref, b_ref, o_ref, *, sigmoid):
    acc = _dot(a_ref[...], b_ref[...].astype(BF16))
    if sigmoid:
        acc = jax.nn.sigmoid(acc)
    o_ref[...] = acc.astype(o_ref.dtype)


def matmul(a, w, layer, *, n_cols, tn, out_dtype=F32, sigmoid=False, name="matmul"):
    M, K = a.shape
    tm = min(M, 1024)
    return pl.pallas_call(
        functools.partial(_mm_kernel, sigmoid=sigmoid), grid=(M // tm, n_cols // tn),
        in_specs=[pl.BlockSpec((tm, K), lambda i, j: (i, 0)),
                  pl.BlockSpec((None, K, tn), lambda i, j: (layer, 0, j))],
        out_specs=pl.BlockSpec((tm, tn), lambda i, j: (i, j)),
        out_shape=jax.ShapeDtypeStruct((M, n_cols), out_dtype),
        compiler_params=_cparams("parallel", "parallel"), name=name,
    )(a, w)


def _mm_swiglu_kernel(a_ref, bg_ref, bu_ref, o_ref):
    a = a_ref[...]
    g = _dot(a, bg_ref[...].astype(BF16))
    u = _dot(a, bu_ref[...].astype(BF16))
    o_ref[...] = (g * jax.nn.sigmoid(g) * u).astype(o_ref.dtype)


def matmul_swiglu(a, w_gu, layer, *, tn=256):
    M, K = a.shape
    F = w_gu.shape[2] // 2
    tm = min(M, 1024)
    nf = F // tn
    return pl.pallas_call(
        _mm_swiglu_kernel, grid=(M // tm, nf),
        in_specs=[pl.BlockSpec((tm, K), lambda i, j: (i, 0)),
                  pl.BlockSpec((None, K, tn), lambda i, j: (layer, 0, j)),
                  pl.BlockSpec((None, K, tn), lambda i, j: (layer, 0, j + nf))],
        out_specs=pl.BlockSpec((tm, tn), lambda i, j: (i, j)),
        out_shape=jax.ShapeDtypeStruct((M, F), BF16),
        compiler_params=_cparams("parallel", "parallel"), name="ffn_gate_up",
    )(a, w_gu, w_gu)


def _mm_res_kernel(a_ref, b_ref, x_ref, o_ref, *, alpha):
    o_ref[...] = x_ref[...] + alpha * _dot(a_ref[...], b_ref[...].astype(BF16))


def matmul_residual(a, w, layer, x, alpha, *, tm, tn, name):
    M, K = a.shape
    N = w.shape[2]
    tm = min(M, tm)
    return pl.pallas_call(
        functools.partial(_mm_res_kernel, alpha=alpha), grid=(M // tm, N // tn),
        in_specs=[pl.BlockSpec((tm, K), lambda i, j: (i, 0)),
                  pl.BlockSpec((None, K, tn), lambda i, j: (layer, 0, j)),
                  pl.BlockSpec((tm, tn), lambda i, j: (i, j))],
        out_specs=pl.BlockSpec((tm, tn), lambda i, j: (i, j)),
        out_shape=jax.ShapeDtypeStruct((M, N), F32),
        compiler_params=_cparams("parallel", "parallel"), name=name,
    )(a, w, x)


def _mm_merge_kernel(od_ref, om_ref, on_ref, wd_ref, wm_ref, wn_ref, g0_ref, g1_ref, g2_ref, o_ref):
    u = g0_ref[...] * _dot(od_ref[...], wd_ref[...])
    u = u + g1_ref[...] * _dot(om_ref[...], wm_ref[...])
    u = u + g2_ref[...] * _dot(on_ref[...], wn_ref[...])
    o_ref[...] = u.astype(o_ref.dtype)


def matmul_merge(o_d, o_m, o_n, w_branch, layer, mg, *, tn=512):
    M = o_d.shape[0]
    D = w_branch.shape[2]
    tm = min(M, 1024)
    wd, wm, wn = o_d.shape[1], o_m.shape[1], o_n.shape[1]
    nd = D // tn
    return pl.pallas_call(
        _mm_merge_kernel, grid=(M // tm, nd),
        in_specs=[pl.BlockSpec((tm, wd), lambda i, j: (i, 0)),
                  pl.BlockSpec((tm, wm), lambda i, j: (i, 0)),
                  pl.BlockSpec((tm, wn), lambda i, j: (i, 0)),
                  pl.BlockSpec((None, wd, tn), lambda i, j: (layer, 0, j)),
                  pl.BlockSpec((None, wm, tn), lambda i, j: (layer, 1, j)),
                  pl.BlockSpec((None, wn, tn), lambda i, j: (layer, 1, j)),
                  pl.BlockSpec((tm, tn), lambda i, j: (i, j)),
                  pl.BlockSpec((tm, tn), lambda i, j: (i, j + nd)),
                  pl.BlockSpec((tm, tn), lambda i, j: (i, j + 2 * nd))],
        out_specs=pl.BlockSpec((tm, tn), lambda i, j: (i, j)),
        out_shape=jax.ShapeDtypeStruct((M, D), BF16),
        compiler_params=_cparams("parallel", "parallel"), name="merge",
    )(o_d, o_m, o_n, w_branch, w_branch, w_branch, mg, mg, mg)


def _post_kernel(p_ref, c128_ref, s128_ref, c64_ref, s64_ref, gd_ref, gm_ref, gn_ref,
                 dq_ref, rd_ref, rdb_ref, mq_ref, rm_ref, rmb_ref, nq_ref, rn_ref, rnb_ref,
                 win_ref, winb_ref, *rest, tm, emit_chunks):
    lane = lax.broadcasted_iota(I32, (tm, LANES), 1)
    lo = lane < 64
    first_half64 = (lane & 63) < 32
    c128, s128 = c128_ref[...], s128_ref[...]
    c64, s64 = c64_ref[...], s64_ref[...]

    def slab(k):
        return p_ref[:, k * LANES:(k + 1) * LANES]

    def nr128(x, g):
        ms = jnp.mean(x * x, axis=-1, keepdims=True)
        y = x * lax.rsqrt(ms + EPS) * g
        return y * c128 + pltpu.roll(y, 64, 1) * s128

    def nr64(x, g):
        x2 = x * x
        s_lo = jnp.sum(jnp.where(lo, x2, 0.0), axis=-1, keepdims=True)
        s_hi = jnp.sum(jnp.where(lo, 0.0, x2), axis=-1, keepdims=True)
        ms = jnp.where(lo, s_lo, s_hi) * (1.0 / 64.0)
        y = x * lax.rsqrt(ms + EPS) * g
        partner = jnp.where(first_half64, pltpu.roll(y, 96, 1), pltpu.roll(y, 32, 1))
        return y * c64 + partner * s64

    def put(ref_f, ref_b, k, val):
        if ref_f is not None:
            ref_f[:, k * LANES:(k + 1) * LANES] = val
        if ref_b is not None:
            ref_b[:, k * LANES:(k + 1) * LANES] = val.astype(BF16)

    gdq, gdk = gd_ref[0:1, :], gd_ref[1:2, :]
    gmq, gmk = gm_ref[0:1, :], gm_ref[1:2, :]
    for h in range(8):
        put(None, dq_ref, h, nr64(slab(h), gdq) * 0.125)
    for h in range(4):
        put(rd_ref, rdb_ref, h, nr64(slab(8 + h), gdk))
        put(rd_ref, rdb_ref, 4 + h, slab(12 + h))
    for h in range(8):
        put(None, mq_ref, h, nr128(slab(16 + h), gmq))
    for h in range(4):
        put(rm_ref, rmb_ref, h, nr128(slab(24 + h), gmk))
        put(rm_ref, rmb_ref, 4 + h, slab(28 + h))
    for h in range(16):
        put(None, nq_ref, h, nr128(slab(32 + h), gn_ref[0:1, :]))
    for g in range(2):
        put(rn_ref, rnb_ref, g, nr128(slab(48 + g), gn_ref[1:2, :]))
        put(rn_ref, rnb_ref, 2 + g, slab(50 + g))
        put(rn_ref, rnb_ref, 4 + g, nr128(slab(52 + g), gn_ref[2:3, :]))
        put(rn_ref, rnb_ref, 6 + g, slab(54 + g))
        put(win_ref, winb_ref, g, nr128(slab(56 + g), gn_ref[3:4, :]))
        put(win_ref, winb_ref, 2 + g, slab(58 + g))
    if emit_chunks:
        xc_ref, slab_ref = rest
        n = tm // CMP_STRIDE
        for cg in range(4):
            slab_ref[...] = rn_ref[:, cg * LANES:(cg + 1) * LANES]
            for j in range(CMP_STRIDE):
                xc_ref[0, cg, :, j * LANES:(j + 1) * LANES] = slab_ref[pl.ds(j, n, stride=CMP_STRIDE), :]


def _rope_tables(pos):
    def tab(d):
        inv = ROPE_THETA ** (-jnp.arange(0, d, 2, dtype=F32) / d)
        ang = pos.astype(F32)[:, None] * inv[None, :]
        cos, sin = jnp.cos(ang), jnp.sin(ang)
        c = jnp.concatenate([cos, cos], axis=-1)
        s = jnp.concatenate([-sin, sin], axis=-1)
        rep = LANES // d
        return jnp.tile(c, (1, rep)), jnp.tile(s, (1, rep))
    c128, s128 = tab(128)
    c64, s64 = tab(64)
    return c128, s128, c64, s64


def post_project(proj, tables, gd, gm, gn, *, seq, batch):
    M = proj.shape[0]
    tm = min(M, 256)
    emit_chunks = seq >= tm
    nt = max(seq // tm, 1)
    row = lambda w: pl.BlockSpec((tm, w), lambda i: (i, 0))
    tab = pl.BlockSpec((tm, LANES), lambda i: (i % nt, 0))
    par = lambda r: pl.BlockSpec((r, LANES), lambda i: (0, 0))
    out_shapes = [
        jax.ShapeDtypeStruct((M, 1024), BF16),
        jax.ShapeDtypeStruct((M, 1024), F32), jax.ShapeDtypeStruct((M, 1024), BF16),
        jax.ShapeDtypeStruct((M, 1024), BF16),
        jax.ShapeDtypeStruct((M, 1024), F32), jax.ShapeDtypeStruct((M, 1024), BF16),
        jax.ShapeDtypeStruct((M, 2048), BF16),
        jax.ShapeDtypeStruct((M, 1024), F32), jax.ShapeDtypeStruct((M, 1024), BF16),
        jax.ShapeDtypeStruct((M, 512), F32), jax.ShapeDtypeStruct((M, 512), BF16),
    ]
    out_specs = [row(1024), row(1024), row(1024), row(1024), row(1024), row(1024), row(2048),
                 row(1024), row(1024), row(512), row(512)]
    if emit_chunks:
        nchunk = seq // CMP_STRIDE
        out_shapes.append(jax.ShapeDtypeStruct((batch, 4, nchunk, CMP_STRIDE * LANES), F32))
        out_specs.append(pl.BlockSpec((1, 4, tm // CMP_STRIDE, CMP_STRIDE * LANES),
                                      lambda i: (i // nt, 0, i % nt, 0)))
    return pl.pallas_call(
        functools.partial(_post_kernel, tm=tm, emit_chunks=emit_chunks), grid=(M // tm,),
        in_specs=[row(QKV_COLS), tab, tab, tab, tab, par(2), par(2), par(4)],
        out_specs=out_specs, out_shape=out_shapes,
        scratch_shapes=[pltpu.VMEM((tm, LANES), F32)] if emit_chunks else [],
        compiler_params=_cparams("parallel"), name="post_project",
    )(proj, *tables, gd, gm, gn)


def _online(carry, s, v, ok=None):
    m, l, acc = carry
    m_new = jnp.maximum(m, jnp.max(s, axis=-1, keepdims=True))
    alpha = jnp.exp(m - m_new)
    p = jnp.exp(s - m_new)
    if ok is not None:
        p = jnp.where(ok, p, 0.0)
    l = alpha * l + jnp.sum(p, axis=-1, keepdims=True)
    acc = alpha * acc + _dot(p.astype(BF16), v)
    return m_new, l, acc


def _init_carry(rows, width):
    return (jnp.full((rows, 1), NEG, F32), jnp.zeros((rows, 1), F32), jnp.zeros((rows, width), F32))


def _flash_reset(m_ref, l_ref, acc_ref):
    m_ref[...] = jnp.full(m_ref.shape, NEG, F32)
    l_ref[...] = jnp.zeros(l_ref.shape, F32)
    acc_ref[...] = jnp.zeros(acc_ref.shape, F32)


def _lambda(lam_ref, lam_init):
    lq = lam_ref[...]
    a = jnp.sum(lq[0:1, :] * lq[1:2, :], axis=-1, keepdims=True)
    b = jnp.sum(lq[2:3, :] * lq[3:4, :], axis=-1, keepdims=True)
    return jnp.exp(a) - jnp.exp(b) + lam_init


def _head_norm(o, g, post):
    ms = jnp.mean(o * o, axis=-1, keepdims=True)
    return o * lax.rsqrt(ms + EPS) * g * post


def _diff_prompt_kernel(lam_ref, gn_ref, q_ref, k_ref, v_ref, o_ref, *, tq, lam_init):
    i = pl.program_id(2)
    lane = lax.broadcasted_iota(I32, (tq, LANES), 1)
    q = q_ref[...]
    zero = jnp.zeros((tq, LANES), BF16)
    parts = []
    for r in range(2):
        qr = q[:, r * LANES:(r + 1) * LANES]
        parts.append(jnp.where(lane < 64, qr, zero))
        parts.append(jnp.where(lane < 64, zero, qr))
    Q = jnp.concatenate(parts, axis=0)
    R = 4 * tq

    def blk(kb):
        s0 = pl.multiple_of(kb * tq, tq)
        return k_ref[pl.ds(s0, tq), :], v_ref[pl.ds(s0, tq), :]

    def body(kb, carry):
        k, v = blk(kb)
        return _online(carry, _dot_nt(Q, k), v)

    carry = lax.fori_loop(0, i, body, _init_carry(R, LANES))
    k, v = blk(i)
    s = _dot_nt(Q, k)
    rowq = lax.broadcasted_iota(I32, (R, tq), 0) & (tq - 1)
    col = lax.broadcasted_iota(I32, (R, tq), 1)
    s = jnp.where(col <= rowq, s, NEG)
    m, l, acc = _online(carry, s, v)
    o = acc / jnp.maximum(l, TINY)
    lam = _lambda(lam_ref, lam_init)
    g = gn_ref[...]
    for r in range(2):
        o_r = o[(2 * r) * tq:(2 * r + 1) * tq] - lam * o[(2 * r + 1) * tq:(2 * r + 2) * tq]
        o_ref[:, r * LANES:(r + 1) * LANES] = _head_norm(o_r, g, 1.0 - lam_init).astype(o_ref.dtype)


def diff_prompt(dq, rows_b, lam_param, out_norm, lam_init, *, batch, seq):
    M = dq.shape[0]
    tq = 256
    nq = seq // tq
    return pl.pallas_call(
        functools.partial(_diff_prompt_kernel, tq=tq, lam_init=lam_init),
        grid=(batch, DIFF_KV_HEADS, nq),
        in_specs=[pl.BlockSpec((4, DIFF_DH), lambda b, g, i: (0, 0)),
                  pl.BlockSpec((1, LANES), lambda b, g, i: (0, 0)),
                  pl.BlockSpec((tq, 2 * LANES), lambda b, g, i: (b * nq + i, g)),
                  pl.BlockSpec((seq, LANES), lambda b, g, i: (b, g)),
                  pl.BlockSpec((seq, LANES), lambda b, g, i: (b, DIFF_KV_HEADS + g))],
        out_specs=pl.BlockSpec((tq, 2 * LANES), lambda b, g, i: (b * nq + i, g)),
        out_shape=jax.ShapeDtypeStruct((M, DIFF_HEADS * LANES), BF16),
        compiler_params=_cparams("parallel", "parallel", "parallel"), name="diff_prompt",
    )(lam_param, out_norm.reshape(1, LANES), dq, rows_b, rows_b)


def _rank_lt(vals, n_cand, lane):
    cnt = jnp.zeros(vals.shape, F32)
    for m in range(n_cand):
        col = vals[:, m:m + 1]
        beats = (col > vals) | ((col == vals) & (lane > m))
        cnt = cnt + jnp.where(beats, 1.0, 0.0)
    return cnt


def _moba_prompt_kernel(q_ref, kf_ref, k_ref, v_ref, o_ref, *, tq, nb):
    i = pl.program_id(2)
    q = q_ref[...]
    Q = jnp.concatenate([q[:, :LANES], q[:, LANES:]], axis=0)
    R = 2 * tq
    lane = lax.broadcasted_iota(I32, (R, LANES), 1)
    kms = [jnp.mean(kf_ref[n * MOBA_BLOCK:(n + 1) * MOBA_BLOCK, :], axis=0, keepdims=True) for n in range(nb)]
    kmean = jnp.concatenate(kms + [jnp.zeros((LANES - nb, LANES), F32)], axis=0)
    gate = _dot_nt(Q, kmean.astype(BF16))
    gate = jnp.where(lane < i, gate, NEG)
    cnt = _rank_lt(gate, nb, lane)
    sel = jnp.where((cnt < MOBA_TOPK) & (gate > NEG / 2), 1.0, 0.0)

    def blk(kb):
        s0 = pl.multiple_of(kb * tq, tq)
        return k_ref[pl.ds(s0, tq), :], v_ref[pl.ds(s0, tq), :]

    def body(kb, carry):
        k, v = blk(kb)
        on = jnp.sum(jnp.where(lane == kb, sel, 0.0), axis=-1, keepdims=True) > 0.5
        s = jnp.where(on, _dot_nt(Q, k) * SCALE128, NEG)
        return _online(carry, s, v, ok=on)

    carry = lax.fori_loop(0, i, body, _init_carry(R, LANES))
    k, v = blk(i)
    rowq = lax.broadcasted_iota(I32, (R, tq), 0) & (tq - 1)
    col = lax.broadcasted_iota(I32, (R, tq), 1)
    s = jnp.where(col <= rowq, _dot_nt(Q, k) * SCALE128, NEG)
    m, l, acc = _online(carry, s, v)
    o = acc / jnp.maximum(l, TINY)
    for r in range(2):
        o_ref[:, r * LANES:(r + 1) * LANES] = o[r * tq:(r + 1) * tq].astype(o_ref.dtype)


def moba_prompt(mq, rows_f, rows_b, *, batch, seq):
    M = mq.shape[0]
    tq = MOBA_BLOCK
    nq = seq // tq
    return pl.pallas_call(
        functools.partial(_moba_prompt_kernel, tq=tq, nb=seq // MOBA_BLOCK),
        grid=(batch, MOBA_KV_HEADS, nq),
        in_specs=[pl.BlockSpec((tq, 2 * LANES), lambda b, g, i: (b * nq + i, g)),
                  pl.BlockSpec((seq, LANES), lambda b, g, i: (b, g)),
                  pl.BlockSpec((seq, LANES), lambda b, g, i: (b, g)),
                  pl.BlockSpec((seq, LANES), lambda b, g, i: (b, MOBA_KV_HEADS + g))],
        out_specs=pl.BlockSpec((tq, 2 * LANES), lambda b, g, i: (b * nq + i, g)),
        out_shape=jax.ShapeDtypeStruct((M, MOBA_HEADS * LANES), BF16),
        compiler_params=_cparams("parallel", "parallel", "parallel"), name="moba_prompt",
    )(mq, rows_f, rows_b, rows_b)


def _compress_kernel(x_ref, pe_ref, w0_ref, w1_ref, w2_ref, o_ref, *, n):
    x = x_ref[0, 0]
    y0 = _dot((x + pe_ref[0, 0:1, :]).astype(BF16), w0_ref[0])
    y1 = _dot((x + pe_ref[0, 1:2, :]).astype(BF16), w1_ref[0])
    hid = y0 + pltpu.roll(y1, n - 1, 0)
    act = hid * jax.nn.sigmoid(hid)
    o_ref[0, 0] = _dot(act.astype(BF16), w2_ref[0]).astype(o_ref.dtype)


def compress(xc, pe, w0, w1, w2):
    B, _, n, W = xc.shape
    return pl.pallas_call(
        functools.partial(_compress_kernel, n=n), grid=(B, 4),
        in_specs=[pl.BlockSpec((1, 1, n, W), lambda b, c: (b, c, 0, 0)),
                  pl.BlockSpec((1, 2, W), lambda b, c: (c // 2, 0, 0)),
                  pl.BlockSpec((1, W, CMP_HIDDEN), lambda b, c: (c // 2, 0, 0)),
                  pl.BlockSpec((1, W, CMP_HIDDEN), lambda b, c: (c // 2, 0, 0)),
                  pl.BlockSpec((1, CMP_HIDDEN, LANES), lambda b, c: (c // 2, 0, 0))],
        out_specs=pl.BlockSpec((1, 1, n, LANES), lambda b, c: (b, c, 0, 0)),
        out_shape=jax.ShapeDtypeStruct((B, 4, n, LANES), BF16),
        compiler_params=_cparams("parallel", "parallel"), name="compress",
    )(xc, pe, w0, w1, w2)


def _nsa_prompt_kernel(q_ref, g_ref, ck_ref, cv_ref, sk_ref, sv_ref, wk_ref, wv_ref, imp_ref, exp_ref,
                       o_ref, msel_ref, *, tq, seq):
    i = pl.program_id(2)
    H = NSA_REP
    R = H * tq
    n_cmp = seq // CMP_STRIDE - 1
    n_sel = seq // SEL_BLOCK
    q = q_ref[...]
    Q = jnp.concatenate([q[:, r * LANES:(r + 1) * LANES] for r in range(H)], axis=0)
    lane3 = lax.broadcasted_iota(I32, (1, tq, LANES), 2)
    qp3 = i * tq + lax.broadcasted_iota(I32, (1, tq, LANES), 1)

    s = (_dot_nt(Q, ck_ref[0, 0]) * SCALE128).reshape(H, tq, LANES)
    ok = (lane3 < n_cmp) & (lane3 * CMP_STRIDE + (CMP_BLOCK - 1) <= qp3)
    s = jnp.where(ok, s, NEG)
    p = jnp.where(ok, jnp.exp(s - jnp.max(s, axis=-1, keepdims=True)), 0.0)
    pc = p / jnp.maximum(jnp.sum(p, axis=-1, keepdims=True), TINY)
    o_c = _dot(pc.reshape(R, LANES).astype(BF16), cv_ref[0, 0])

    pcs = jnp.sum(pc, axis=0)
    imp = _dot_exact(pcs, imp_ref[...])
    lane = lane3[0]
    cur = qp3[0] >> 6
    valid = lane <= cur
    forced = (lane == 0) | (lane == cur) | (lane == cur - 1)
    imp = jnp.where(valid, jnp.where(forced, FORCE_SCORE, imp), NEG)
    cnt = _rank_lt(imp, n_sel, lane)
    sel = jnp.where((cnt < SEL_TOPK) & (imp > NEG / 2), 1.0, 0.0)
    msel_ref[...] = _dot(sel.astype(BF16), exp_ref[...])

    def blk(kref, vref, kb):
        s0 = pl.multiple_of(kb * tq, tq)
        return kref[pl.ds(s0, tq), :], vref[pl.ds(s0, tq), :]

    def attend(carry, kb, k, v, ok3):
        s3 = (_dot_nt(Q, k) * SCALE128).reshape(H, tq, tq)
        return _online(carry, jnp.where(ok3, s3, NEG).reshape(R, tq), v)

    def sel_body(kb, carry):
        k, v = blk(sk_ref, sv_ref, kb)
        on = msel_ref[:, pl.ds(pl.multiple_of(kb * tq, tq), tq)] > 0.5
        keypos = kb * tq + lane3
        return attend(carry, kb, k, v, on[None] & (keypos <= qp3))

    m, l, acc = lax.fori_loop(0, i + 1, sel_body, _init_carry(R, LANES))
    o_s = acc / jnp.maximum(l, TINY)

    def win_body(kb, carry):
        k, v = blk(wk_ref, wv_ref, kb)
        dist = qp3 - (kb * tq + lane3)
        return attend(carry, kb, k, v, (dist >= 0) & (dist <= WINDOW))

    m, l, acc = lax.fori_loop(jnp.maximum(i - WINDOW // tq, 0), i + 1, win_body, _init_carry(R, LANES))
    o_w = acc / jnp.maximum(l, TINY)

    gates = g_ref[...]
    for r in range(H):
        rs = slice(r * tq, (r + 1) * tq)
        o = (gates[:, r:r + 1] * o_c[rs] + gates[:, H + r:H + r + 1] * o_s[rs]
             + gates[:, 2 * H + r:2 * H + r + 1] * o_w[rs])
        o_ref[:, r * LANES:(r + 1) * LANES] = o.astype(o_ref.dtype)


def _importance_matrix(n_tok_pad, n_tok, n_blk_pad, n_blk):
    r_ = SEL_BLOCK // CMP_STRIDE
    f_ = CMP_BLOCK // CMP_STRIDE - 1
    i = np.arange(n_tok_pad)[:, None]
    j = np.arange(n_blk_pad)[None, :]
    a = (i >= r_ * j - f_) & (i <= r_ * j + r_ - 1) & (i < n_tok) & (j < n_blk)
    return jnp.asarray(a.astype(np.float32))


def _expand_matrix(n_blk_pad, n_keys):
    j = np.arange(n_blk_pad)[:, None]
    t = np.arange(n_keys)[None, :]
    return jnp.asarray((t // SEL_BLOCK == j).astype(np.float32), dtype=BF16)


def nsa_prompt(nq, ng, cmp_tok, rows_b, win_b, *, batch, seq):
    M = nq.shape[0]
    tq = 128
    nt = seq // tq
    G = NSA_KV_HEADS
    n_tok = seq // CMP_STRIDE
    assert n_tok == LANES and seq // SEL_BLOCK <= LANES
    imp_m = _importance_matrix(LANES, n_tok - 1, LANES, seq // SEL_BLOCK)
    exp_m = _expand_matrix(LANES, seq)
    kv = lambda c: pl.BlockSpec((seq, LANES), lambda b, g, i: (b, c + g))
    return pl.pallas_call(
        functools.partial(_nsa_prompt_kernel, tq=tq, seq=seq), grid=(batch, G, nt),
        in_specs=[pl.BlockSpec((tq, NSA_REP * LANES), lambda b, g, i: (b * nt + i, g)),
                  pl.BlockSpec((tq, LANES), lambda b, g, i: (b * nt + i, g)),
                  pl.BlockSpec((1, 1, n_tok, LANES), lambda b, g, i: (b, g, 0, 0)),
                  pl.BlockSpec((1, 1, n_tok, LANES), lambda b, g, i: (b, G + g, 0, 0)),
                  kv(4), kv(6), kv(0), kv(2),
                  pl.BlockSpec((LANES, LANES), lambda b, g, i: (0, 0)),
                  pl.BlockSpec((LANES, seq), lambda b, g, i: (0, 0))],
        out_specs=pl.BlockSpec((tq, NSA_REP * LANES), lambda b, g, i: (b * nt + i, g)),
        out_shape=jax.ShapeDtypeStruct((M, NSA_HEADS * LANES), BF16),
        scratch_shapes=[pltpu.VMEM((tq, seq), F32)],
        compiler_params=_cparams("parallel", "parallel", "arbitrary"), name="nsa_prompt",
    )(nq, ng, cmp_tok, cmp_tok, rows_b, rows_b, win_b, win_b, imp_m, exp_m)


def _slot_scores(q, raw_bf16, key_slot, nslots):
    s = _dot_nt(q, raw_bf16)
    col = lax.broadcasted_iota(I32, s.shape, 1) & (nslots - 1)
    return s, col == key_slot


def _slot_online(m_ref, l_ref, acc_ref, s, ok, raw_bf16, shift):
    m_old = m_ref[...]
    m_new = jnp.maximum(m_old, jnp.max(s, axis=-1, keepdims=True))
    alpha = jnp.exp(m_old - m_new)
    p = jnp.where(ok, jnp.exp(s - m_new), 0.0)
    l_ref[...] = alpha * l_ref[...] + jnp.sum(p, axis=-1, keepdims=True)
    acc_ref[...] = alpha * acc_ref[...] + _dot(pltpu.roll(p, shift, 1).astype(BF16), raw_bf16)
    m_ref[...] = m_new


def _with_new(m, l, acc, s_new, vn, on=True):
    s_new = jnp.where(on, s_new, NEG)
    m2 = jnp.maximum(m, s_new)
    a2 = jnp.exp(m - m2)
    pn = jnp.where(on, jnp.exp(s_new - m2), 0.0)
    return (a2 * acc + pn * vn) / jnp.maximum(a2 * l + pn, TINY)


def _diff_sample_kernel(pt_ref, lam_ref, gn_ref, q_ref, kn_ref, vn_ref, *refs, P, lam_init):
    pages = refs[:P]
    o_ref = refs[P]
    m_ref, l_ref, acc_ref = refs[P + 1:]
    j = pl.program_id(1)

    @pl.when(j == 0)
    def _():
        _flash_reset(m_ref, l_ref, acc_ref)

    Q = q_ref[0]
    raw = jnp.concatenate([pages[p][0, 0].astype(BF16) for p in range(P)], axis=0)
    s = _dot_nt(Q, raw)
    slot = lax.broadcasted_iota(I32, s.shape, 1) & (SLOTS - 1)
    grp = lax.broadcasted_iota(I32, s.shape, 0) >> 2
    ok = slot == grp
    _slot_online(m_ref, l_ref, acc_ref, jnp.where(ok, s, NEG), ok, raw, DIFF_KV_HEADS)

    @pl.when(j == pl.num_programs(1) - 1)
    def _():
        s_new = jnp.sum(Q.astype(F32) * kn_ref[0].astype(F32), axis=-1, keepdims=True)
        a = _with_new(m_ref[...], l_ref[...], acc_ref[...], s_new, vn_ref[0].astype(F32))
        lam = _lambda(lam_ref, lam_init)
        g = gn_ref[...]
        outs = []
        for h in range(DIFF_HEADS):
            o = a[2 * h:2 * h + 1] - lam * a[2 * h + 1:2 * h + 2]
            outs.append(_head_norm(o, g, 1.0 - lam_init))
        o_ref[0] = jnp.concatenate(outs, axis=0).astype(o_ref.dtype)


def diff_sample(q16, kn16, vn16, cache, page_table_flat, lam_param, out_norm, lam_init, *, layer, batch, n_pages):
    P = 8
    R = PAGE_SIZE * SLOTS
    page = lambda p: pl.BlockSpec((1, 1, R, LANES), lambda b, j, pt: (layer, pt[b * n_pages + j * P + p], 0, 0))
    per_b = pl.BlockSpec((1, 16, LANES), lambda b, j, pt: (b, 0, 0))
    gs = pltpu.PrefetchScalarGridSpec(
        num_scalar_prefetch=1, grid=(batch, n_pages // P),
        in_specs=[pl.BlockSpec((4, DIFF_DH), lambda b, j, pt: (0, 0)),
                  pl.BlockSpec((1, LANES), lambda b, j, pt: (0, 0)),
                  per_b, per_b, per_b] + [page(p) for p in range(P)],
        out_specs=pl.BlockSpec((1, DIFF_HEADS, LANES), lambda b, j, pt: (b, 0, 0)),
        scratch_shapes=[pltpu.VMEM((16, 1), F32), pltpu.VMEM((16, 1), F32), pltpu.VMEM((16, LANES), F32)])
    return pl.pallas_call(
        functools.partial(_diff_sample_kernel, P=P, lam_init=lam_init), grid_spec=gs,
        out_shape=jax.ShapeDtypeStruct((batch, DIFF_HEADS, LANES), BF16),
        compiler_params=_cparams("parallel", "arbitrary"), name="diff_sample",
    )(page_table_flat, lam_param, out_norm.reshape(1, LANES), q16, kn16, vn16, *([cache] * P))


def _moba_gate_kernel(pt_ref, q_ref, *refs, nblk, bps):
    pages = refs[:2 * bps]
    o_ref, gate_ref = refs[2 * bps:]
    n = pl.program_id(1)
    lane = lax.broadcasted_iota(I32, (MOBA_HEADS, LANES), 1)

    @pl.when(n == 0)
    def _():
        gate_ref[...] = jnp.full(gate_ref.shape, NEG, F32)

    q = q_ref[0].astype(F32)
    gate = gate_ref[...]
    for t in range(bps):
        ksum = (jnp.sum(pages[2 * t][0, 0].reshape(PAGE_SIZE, SLOTS, LANES), axis=0)
                + jnp.sum(pages[2 * t + 1][0, 0].reshape(PAGE_SIZE, SLOTS, LANES), axis=0))
        kmean = (ksum * (1.0 / MOBA_BLOCK)).astype(BF16).astype(F32)
        per_head = jnp.concatenate([kmean[h // 2:h // 2 + 1] for h in range(MOBA_HEADS)], axis=0)
        g = jnp.sum(q * per_head, axis=-1, keepdims=True)
        gate = jnp.where(lane == n * bps + t, g, gate)
    gate_ref[...] = gate

    @pl.when(n == nblk // bps - 1)
    def _():
        cnt = _rank_lt(gate, nblk, lane)
        lanef = lane.astype(F32)
        out = jnp.zeros(gate.shape, F32)
        for k in range(MOBA_TOPK):
            hit = cnt == float(k)
            idx = jnp.sum(jnp.where(hit, lanef, 0.0), axis=-1, keepdims=True)
            val = jnp.max(jnp.where(hit, gate, NEG), axis=-1, keepdims=True)
            out = out + jnp.where(lane == k, idx, 0.0) + jnp.where((lane == 4 + k) & (val > NEG / 2), 1.0, 0.0)
        o_ref[0] = out.astype(I32)


def moba_gate_sample(q, cache, page_table_flat, *, layer, batch, n_pages):
    nblk = n_pages * PAGE_SIZE // MOBA_BLOCK
    ppb = MOBA_BLOCK // PAGE_SIZE
    bps = 4 if nblk % 4 == 0 else 1
    assert ppb == 2 and nblk <= LANES
    R = PAGE_SIZE * SLOTS
    page = lambda p: pl.BlockSpec((1, 1, R, LANES),
                                  lambda b, n, pt: (layer, pt[b * n_pages + n * bps * ppb + p], 0, 0))
    gs = pltpu.PrefetchScalarGridSpec(
        num_scalar_prefetch=1, grid=(batch, nblk // bps),
        in_specs=[pl.BlockSpec((1, MOBA_HEADS, LANES), lambda b, n, pt: (b, 0, 0))]
        + [page(p) for p in range(bps * ppb)],
        out_specs=pl.BlockSpec((1, MOBA_HEADS, LANES), lambda b, n, pt: (b, 0, 0)),
        scratch_shapes=[pltpu.VMEM((MOBA_HEADS, LANES), F32)])
    return pl.pallas_call(
        functools.partial(_moba_gate_kernel, nblk=nblk, bps=bps), grid_spec=gs,
        out_shape=jax.ShapeDtypeStruct((batch, MOBA_HEADS, LANES), I32),
        compiler_params=_cparams("parallel", "arbitrary"), name="moba_gate_sample",
    )(page_table_flat, q, *([cache] * (bps * ppb)))


def _moba_sample_kernel(pg_ref, ok_ref, q_ref, kn_ref, vn_ref, page_ref, o_ref, m_ref, l_ref, acc_ref, *, nstep):
    b, h, t = pl.program_id(0), pl.program_id(1), pl.program_id(2)

    @pl.when(t == 0)
    def _():
        _flash_reset(m_ref, l_ref, acc_ref)

    Q = q_ref[0, 0]
    raw = page_ref[0, 0].astype(BF16)
    on = ok_ref[(b * MOBA_HEADS + h) * MOBA_TOPK + t // 2] > 0
    s, is_key = _slot_scores(Q, raw, h // 2, SLOTS)
    ok = is_key & on
    _slot_online(m_ref, l_ref, acc_ref, jnp.where(ok, s * SCALE128, NEG), ok, raw, MOBA_KV_HEADS)

    @pl.when(t == nstep - 1)
    def _():
        s_new = jnp.sum(Q.astype(F32) * kn_ref[0].astype(F32), axis=-1, keepdims=True) * SCALE128
        o = _with_new(m_ref[...], l_ref[...], acc_ref[...], s_new, vn_ref[0].astype(F32))
        o_ref[0] = o[0:1].astype(o_ref.dtype)


def moba_sample(q8, kv_new, cache, sel_pages, sel_ok, *, layer, batch):
    nstep = MOBA_TOPK * (MOBA_BLOCK // PAGE_SIZE)
    G = MOBA_KV_HEADS
    R = PAGE_SIZE * SLOTS
    gs = pltpu.PrefetchScalarGridSpec(
        num_scalar_prefetch=2, grid=(batch, MOBA_HEADS, nstep),
        in_specs=[pl.BlockSpec((1, 1, 8, LANES), lambda b, h, t, pgs, oks: (b, h, 0, 0)),
                  pl.BlockSpec((1, 1, LANES), lambda b, h, t, pgs, oks: (b, 0, h // 2)),
                  pl.BlockSpec((1, 1, LANES), lambda b, h, t, pgs, oks: (b, 0, G + h // 2)),
                  pl.BlockSpec((1, 1, R, LANES),
                               lambda b, h, t, pgs, oks: (layer, pgs[(b * MOBA_HEADS + h) * nstep + t], 0, 0))],
        out_specs=pl.BlockSpec((1, 1, LANES), lambda b, h, t, pgs, oks: (b, 0, h)),
        scratch_shapes=[pltpu.VMEM((8, 1), F32), pltpu.VMEM((8, 1), F32), pltpu.VMEM((8, LANES), F32)])
    return pl.pallas_call(
        functools.partial(_moba_sample_kernel, nstep=nstep), grid_spec=gs,
        out_shape=jax.ShapeDtypeStruct((batch, 1, MOBA_HEADS * LANES), BF16),
        compiler_params=_cparams("parallel", "parallel", "arbitrary"), name="moba_sample",
    )(sel_pages, sel_ok, q8, kv_new, kv_new, cache)


def _nsa_chunk_kernel(pt_ref, *refs, P):
    pages, o_ref = refs[:P], refs[P]
    n = PAGE_SIZE // CMP_STRIDE
    for p in range(P):
        for cg in range(4):
            for j in range(CMP_STRIDE):
                o_ref[0, cg, p * n:(p + 1) * n, j * LANES:(j + 1) * LANES] = (
                    pages[p][0, 0, pl.ds(j * SLOTS + cg, n, stride=CMP_STRIDE * SLOTS), :])


def nsa_chunk_sample(cache, page_table_flat, *, layer, batch, n_pages):
    n = PAGE_SIZE // CMP_STRIDE
    P = 4
    R = PAGE_SIZE * SLOTS
    page = lambda p: pl.BlockSpec((1, 1, R, LANES), lambda b, s, pt: (layer, pt[b * n_pages + s * P + p], 0, 0))
    gs = pltpu.PrefetchScalarGridSpec(
        num_scalar_prefetch=1, grid=(batch, n_pages // P),
        in_specs=[page(p) for p in range(P)],
        out_specs=pl.BlockSpec((1, 4, P * n, CMP_STRIDE * LANES), lambda b, s, pt: (b, 0, s, 0)))
    return pl.pallas_call(
        functools.partial(_nsa_chunk_kernel, P=P), grid_spec=gs,
        out_shape=jax.ShapeDtypeStruct((batch, 4, n_pages * n, CMP_STRIDE * LANES), F32),
        compiler_params=_cparams("parallel", "parallel"), name="nsa_chunk_sample",
    )(page_table_flat, *([cache] * P))


def _nsa_cmp_sample_kernel(q_ref, ck_ref, cv_ref, imp_ref, oc_ref, sel_ref, *, n_tok, n_valid, cur, n_blk_pad):
    Q = q_ref[0, 0]
    lane = lax.broadcasted_iota(I32, (NSA_REP, n_tok), 1)
    ok = lane < n_valid
    s = jnp.where(ok, _dot_nt(Q, ck_ref[0, 0]) * SCALE128, NEG)
    p = jnp.where(ok, jnp.exp(s - jnp.max(s, axis=-1, keepdims=True)), 0.0)
    pc = p / jnp.maximum(jnp.sum(p, axis=-1, keepdims=True), TINY)
    oc_ref[0, 0] = _dot(pc.astype(BF16), cv_ref[0, 0])
    pcs = jnp.broadcast_to(jnp.sum(pc, axis=0, keepdims=True), (8, n_tok))
    imp = _dot_exact(pcs, imp_ref[...])[0:1]
    blk = lax.broadcasted_iota(I32, (1, n_blk_pad), 1)
    valid = blk <= cur
    forced = (blk == 0) | (blk == cur) | (blk == cur - 1)
    imp = jnp.where(valid, jnp.where(forced, FORCE_SCORE, imp), NEG)
    rows = jnp.broadcast_to(imp, (n_blk_pad, n_blk_pad))
    cols = rows.T
    mi = lax.broadcasted_iota(I32, (n_blk_pad, n_blk_pad), 0)
    ji = lax.broadcasted_iota(I32, (n_blk_pad, n_blk_pad), 1)
    beats = (cols > rows) | ((cols == rows) & (mi < ji))
    cnt = jnp.sum(jnp.where(beats, 1.0, 0.0), axis=0, keepdims=True)
    blkf = blk.astype(F32)
    lane_o = lax.broadcasted_iota(I32, (1, LANES), 1)
    out = jnp.zeros((1, LANES), F32)
    for k in range(SEL_TOPK):
        hit = cnt == float(k)
        idx = jnp.sum(jnp.where(hit, blkf, 0.0), axis=-1, keepdims=True)
        val = jnp.max(jnp.where(hit, imp, NEG), axis=-1, keepdims=True)
        out = out + jnp.where(lane_o == k, idx, 0.0) + jnp.where((lane_o == SEL_TOPK + k) & (val > NEG / 2), 1.0, 0.0)
    sel_ref[0, 0] = jnp.broadcast_to(out, (8, LANES)).astype(I32)


def nsa_cmp_sample(q8, cmp_tok, *, batch, past_len):
    G = NSA_KV_HEADS
    n_tok = cmp_tok.shape[2]
    n_valid = n_tok - 1
    cur = past_len // SEL_BLOCK
    n_blk = cur + 1
    n_blk_pad = -(-n_blk // LANES) * LANES
    imp_m = _importance_matrix(n_tok, n_tok, n_blk_pad, n_blk)
    return pl.pallas_call(
        functools.partial(_nsa_cmp_sample_kernel, n_tok=n_tok, n_valid=n_valid, cur=cur, n_blk_pad=n_blk_pad),
        grid=(batch, G),
        in_specs=[pl.BlockSpec((1, 1, NSA_REP, LANES), lambda b, g: (b, g, 0, 0)),
                  pl.BlockSpec((1, 1, n_tok, LANES), lambda b, g: (b, g, 0, 0)),
                  pl.BlockSpec((1, 1, n_tok, LANES), lambda b, g: (b, G + g, 0, 0)),
                  pl.BlockSpec((n_tok, n_blk_pad), lambda b, g: (0, 0))],
        out_specs=[pl.BlockSpec((1, 1, NSA_REP, LANES), lambda b, g: (b, g, 0, 0)),
                   pl.BlockSpec((1, 1, 8, LANES), lambda b, g: (b, g, 0, 0))],
        out_shape=[jax.ShapeDtypeStruct((batch, G, NSA_REP, LANES), F32),
                   jax.ShapeDtypeStruct((batch, G, 8, LANES), I32)],
        compiler_params=_cparams("parallel", "parallel"), name="nsa_cmp_sample",
    )(q8, cmp_tok, cmp_tok, imp_m)


def _nsa_sel_sample_kernel(pg_ref, hb_ref, ok_ref, own_ref, q_ref, gt_ref, oc_ref, kn_ref, vn_ref, wkn_ref,
                           wvn_ref, win_ref, blk_ref, o_ref, m_ref, l_ref, acc_ref):
    b, g, t = pl.program_id(0), pl.program_id(1), pl.program_id(2)
    G = NSA_KV_HEADS

    @pl.when(t == 0)
    def _():
        _flash_reset(m_ref, l_ref, acc_ref)

    Q = q_ref[0, 0]
    raw = blk_ref[0, 0].astype(BF16)
    on = ok_ref[(b * G + g) * SEL_TOPK + t] > 0
    s, is_key = _slot_scores(Q, raw, 2 * G + g, SLOTS)
    ok = is_key & on
    _slot_online(m_ref, l_ref, acc_ref, jnp.where(ok, s * SCALE128, NEG), ok, raw, G)

    @pl.when(t == SEL_TOPK - 1)
    def _():
        Qf = Q.astype(F32)

        def new_score(kn_ref):
            return jnp.sum(Qf * kn_ref[0].astype(F32), axis=-1, keepdims=True) * SCALE128

        own = own_ref[b * G + g] > 0
        o_s = _with_new(m_ref[...], l_ref[...], acc_ref[...], new_score(kn_ref), vn_ref[0].astype(F32), own)
        raww = win_ref[0, 0].astype(BF16)
        sw, okw = _slot_scores(Q, raww, g, 2 * G)
        sw = jnp.where(okw, sw * SCALE128, NEG)
        mw = jnp.max(sw, axis=-1, keepdims=True)
        pw = jnp.where(okw, jnp.exp(sw - mw), 0.0)
        lw = jnp.sum(pw, axis=-1, keepdims=True)
        aw = _dot(pltpu.roll(pw, G, 1).astype(BF16), raww)
        o_w = _with_new(mw, lw, aw, new_score(wkn_ref), wvn_ref[0].astype(F32))
        gt = gt_ref[0, 0]
        o_ref[0, 0] = gt[:, 0:1] * oc_ref[0, 0] + gt[:, 1:2] * o_s + gt[:, 2:3] * o_w


def nsa_sel_sample(q8, gates8, o_c, rows_new, win_new, state_win, cache, sel_pages, sel_half, sel_ok, sel_own, *,
                   layer, batch):
    G = NSA_KV_HEADS
    wrows = state_win.shape[2]
    idx = lambda b, g, t: (b * G + g) * SEL_TOPK + t
    new = lambda c: pl.BlockSpec((1, 1, LANES), lambda b, g, t, pgs, hbs, oks, own: (b, 0, c + g))
    per_bg = pl.BlockSpec((1, 1, NSA_REP, LANES), lambda b, g, t, pgs, hbs, oks, own: (b, g, 0, 0))
    gs = pltpu.PrefetchScalarGridSpec(
        num_scalar_prefetch=4, grid=(batch, G, SEL_TOPK),
        in_specs=[per_bg, per_bg, per_bg, new(4), new(6), new(0), new(2),
                  pl.BlockSpec((1, 1, wrows, LANES), lambda b, g, t, pgs, hbs, oks, own: (layer, b, 0, 0)),
                  pl.BlockSpec((1, 1, SEL_BLOCK * SLOTS, LANES),
                               lambda b, g, t, pgs, hbs, oks, own: (layer, pgs[idx(b, g, t)], hbs[idx(b, g, t)], 0))],
        out_specs=per_bg,
        scratch_shapes=[pltpu.VMEM((8, 1), F32), pltpu.VMEM((8, 1), F32), pltpu.VMEM((8, LANES), F32)])
    return pl.pallas_call(
        _nsa_sel_sample_kernel, grid_spec=gs,
        out_shape=jax.ShapeDtypeStruct((batch, G, NSA_REP, LANES), F32),
        compiler_params=_cparams("parallel", "parallel", "arbitrary"), name="nsa_sel_sample",
    )(sel_pages, sel_half, sel_ok, sel_own, q8, gates8, o_c, rows_new, rows_new, win_new, win_new,
      state_win, cache)


def _prep_weights(w_in, ffn1_w_down, ffn2_w_down, w_branch, w_out, nsa_cmp_w1, nsa_cmp_w2, nsa_cmp_pos):
    depth = w_in.shape[0]
    bf = lambda w: w.astype(BF16)
    src = np.zeros((NSA_KV_HEADS, 3, NSA_REP), np.int32)
    for g in range(NSA_KV_HEADS):
        for t in range(3):
            for r in range(NSA_REP):
                src[g, t, r] = (g * NSA_REP + r) * 3 + t
    w_ng = w_in[:, :, QKV_COLS:QKV_COLS + NG_COLS][:, :, src.reshape(NSA_KV_HEADS, 3 * NSA_REP)]
    w_ng = jnp.pad(w_ng, ((0, 0), (0, 0), (0, 0), (0, LANES - 3 * NSA_REP))).reshape(depth, D_MODEL, NG_PAD)
    W = CMP_STRIDE * HEAD_DIM
    return dict(
        down1=bf(ffn1_w_down), down2=bf(ffn2_w_down), ng=bf(w_ng), mg=bf(w_in[:, :, QKV_COLS + NG_COLS:]),
        branch=bf(w_branch), out=bf(w_out),
        cw0=bf(nsa_cmp_w1[:, :, :CMP_STRIDE].reshape(depth, 2, W, CMP_HIDDEN)),
        cw1=bf(nsa_cmp_w1[:, :, CMP_STRIDE:].reshape(depth, 2, W, CMP_HIDDEN)),
        cw2=bf(nsa_cmp_w2),
        cpe=nsa_cmp_pos.reshape(depth, 2, 2, W),
    )


def _ffn(x, norm_g, w_gu, w_down, layer):
    h = rmsnorm_bf16(x, norm_g)
    act = matmul_swiglu(h, w_gu, layer)
    return matmul_residual(act, w_down, layer, x, 0.5, tm=512, tn=256, name="ffn_down")


def _project(x, mix_g, w_in, wp, layer, tables, gd, gm, gn, *, seq, batch):
    h = rmsnorm_bf16(x, mix_g)
    proj = matmul(h, w_in, layer, n_cols=QKV_COLS, tn=512, name="in_proj_qkv")
    ng = matmul(h, wp['ng'], layer, n_cols=NG_PAD, tn=NG_PAD, sigmoid=True, name="in_proj_gate")
    mg = matmul(h, wp['mg'], layer, n_cols=MG_COLS, tn=512, sigmoid=True, name="in_proj_merge_gate")
    post = post_project(proj, tables, gd, gm, gn, seq=seq, batch=batch)
    return post, ng, mg


def _merge(x, o_d, o_m, o_n, mg, wp, layer):
    u = matmul_merge(o_d, o_m, o_n, wp['branch'], layer, mg)
    return matmul_residual(u, wp['out'], layer, x, 1.0, tm=1024, tn=512, name="out_proj")


def kernel(x_prompt, x_sample, cache_diff, cache_moba, cache_nsa, state_nsa_win, page_table, ffn1_norm, ffn1_w_gu, ffn1_w_down, mix_norm, w_in, diff_qk_norm, diff_lambda, diff_out_norm, moba_qk_norm, nsa_qk_norm, nsa_cmp_w1, nsa_cmp_w2, nsa_cmp_pos, w_branch, w_out, ffn2_norm, ffn2_w_gu, ffn2_w_down):
    B, T, D = x_prompt.shape
    Bs = x_sample.shape[0]
    depth = ffn1_norm.shape[0]
    n_pool = cache_diff.shape[1]
    n_pages = page_table.shape[1]
    past_len = n_pages * PAGE_SIZE
    MS = 16
    assert x_sample.shape[1] == 1 and Bs <= MS and T % MOBA_BLOCK == 0

    xp = x_prompt.reshape(B * T, D)
    xs = jnp.pad(x_sample.reshape(Bs, D), ((0, MS - Bs), (0, 0)))
    c_diff = cache_diff.reshape(depth, n_pool, PAGE_SIZE * SLOTS, HEAD_DIM)
    c_moba = cache_moba.reshape(depth, n_pool, PAGE_SIZE * SLOTS, HEAD_DIM)
    c_nsa = cache_nsa.reshape(depth, n_pool, PAGE_SIZE * SLOTS, HEAD_DIM)
    wlen = state_nsa_win.shape[2]
    s_win = state_nsa_win.reshape(depth, Bs, wlen * 2 * NSA_KV_HEADS, HEAD_DIM)
    pt_flat = page_table.reshape(-1).astype(I32)
    tab_p = _rope_tables(jnp.arange(T, dtype=I32))
    tab_s = _rope_tables(jnp.full((MS,), past_len, I32))
    wp = _prep_weights(w_in, ffn1_w_down, ffn2_w_down, w_branch, w_out, nsa_cmp_w1, nsa_cmp_w2, nsa_cmp_pos)

    outs = [[] for _ in range(8)]
    for l in range(depth):
        lam_init = 0.8 - 0.6 * math.exp(-0.3 * l)
        gd = jnp.tile(diff_qk_norm[l], (1, 2))
        gm, gn = moba_qk_norm[l], nsa_qk_norm[l]
        lam_p = diff_lambda[l]
        cmp_w = (wp['cpe'][l], wp['cw0'][l], wp['cw1'][l], wp['cw2'][l])

        xp = _ffn(xp, ffn1_norm[l], ffn1_w_gu, wp['down1'], l)
        xs = _ffn(xs, ffn1_norm[l], ffn1_w_gu, wp['down1'], l)

        post, ng, mg = _project(xp, mix_norm[l], w_in, wp, l, tab_p, gd, gm, gn, seq=T, batch=B)
        dq, rd, rdb, mq, rm, rmb, nq, rn, rnb, win, winb, xc = post
        o_d = diff_prompt(dq, rdb, lam_p, diff_out_norm[l], lam_init, batch=B, seq=T)
        o_m = moba_prompt(mq, rm, rmb, batch=B, seq=T)
        cmp_tok = compress(xc, *cmp_w)
        o_n = nsa_prompt(nq, ng, cmp_tok, rnb, winb, batch=B, seq=T)
        xp = _merge(xp, o_d, o_m, o_n, mg, wp, l)
        outs[0].append(rd.reshape(B, T, 2, DIFF_KV_HEADS, HEAD_DIM))
        outs[2].append(rm.reshape(B, T, 2, MOBA_KV_HEADS, HEAD_DIM))
        outs[4].append(rn.reshape(B, T, 4, NSA_KV_HEADS, HEAD_DIM))
        wkeep = min(WINDOW, T)
        outs[6].append(win.reshape(B, T, 2, NSA_KV_HEADS, HEAD_DIM)[:, T - wkeep:])

        post, ng, mg = _project(xs, mix_norm[l], w_in, wp, l, tab_s, gd, gm, gn, seq=1, batch=Bs)
        dq, rd, rdb, mq, rm, rmb, nq, rn, rnb, win, winb = post
        q5 = dq[:Bs].reshape(Bs, DIFF_KV_HEADS, 2, 2, DIFF_DH)
        eye_c = jnp.eye(2, dtype=BF16)
        q16 = (q5[:, :, :, :, None, :] * eye_c[None, None, None, :, :, None]).reshape(Bs, 16, LANES)
        kv4 = rdb[:Bs].reshape(Bs, 2, DIFF_KV_HEADS, 1, LANES)
        kvn16 = jnp.broadcast_to(kv4, (Bs, 2, DIFF_KV_HEADS, 4, LANES)).reshape(Bs, 2, 16, LANES)
        o_d = diff_sample(q16, kvn16[:, 0], kvn16[:, 1], c_diff, pt_flat, lam_p, diff_out_norm[l], lam_init,
                          layer=l, batch=Bs, n_pages=n_pages)
        msel = moba_gate_sample(mq[:Bs].reshape(Bs, MOBA_HEADS, LANES), c_moba, pt_flat,
                                layer=l, batch=Bs, n_pages=n_pages)
        m_idx = msel[:, :, 0:MOBA_TOPK]
        m_ok = msel[:, :, 4:4 + MOBA_TOPK]
        ppb = MOBA_BLOCK // PAGE_SIZE
        logical = (m_idx[..., None] * ppb + jnp.arange(ppb, dtype=I32)).reshape(Bs, MOBA_HEADS * MOBA_TOPK * ppb)
        m_pages = jnp.take_along_axis(page_table.astype(I32), logical, axis=1)
        q8 = jnp.broadcast_to(mq[:Bs].reshape(Bs, MOBA_HEADS, 1, HEAD_DIM), (Bs, MOBA_HEADS, 8, HEAD_DIM))
        o_m = moba_sample(q8, rmb[:Bs].reshape(Bs, 1, 1024), c_moba, m_pages.reshape(-1), m_ok.reshape(-1),
                          layer=l, batch=Bs)
        xcs = nsa_chunk_sample(c_nsa, pt_flat, layer=l, batch=Bs, n_pages=n_pages)
        cmp_tok = compress(xcs, *cmp_w)
        nq8 = nq[:Bs].reshape(Bs, NSA_KV_HEADS, NSA_REP, HEAD_DIM)
        o_c, nsel = nsa_cmp_sample(nq8, cmp_tok, batch=Bs, past_len=past_len)
        n_idx = nsel[:, :, 0, 0:SEL_TOPK]
        n_ok = nsel[:, :, 0, SEL_TOPK:2 * SEL_TOPK]
        bpp = PAGE_SIZE // SEL_BLOCK
        in_past = n_idx < past_len // SEL_BLOCK
        n_log = jnp.where(in_past, n_idx // bpp, 0)
        n_pages_sel = jnp.take_along_axis(page_table.astype(I32), n_log.reshape(Bs, -1), axis=1)
        n_half = jnp.where(in_past, n_idx % bpp, 0)
        n_on = (n_ok > 0) & in_past
        n_own = jnp.any((n_ok > 0) & (n_idx == past_len // SEL_BLOCK), axis=-1)
        gates8 = jnp.pad(ng[:Bs].reshape(Bs, NSA_KV_HEADS, LANES)[:, :, :3 * NSA_REP]
                         .reshape(Bs, NSA_KV_HEADS, 3, NSA_REP).transpose(0, 1, 3, 2),
                         ((0, 0), (0, 0), (0, 0), (0, LANES - 3)))
        o_n = nsa_sel_sample(nq8, gates8, o_c, rnb[:Bs].reshape(Bs, 1, 1024), winb[:Bs].reshape(Bs, 1, 512),
                             s_win, c_nsa, n_pages_sel.reshape(-1), n_half.reshape(-1).astype(I32),
                             n_on.reshape(-1).astype(I32), n_own.reshape(-1).astype(I32), layer=l, batch=Bs)
        pad_rows = lambda a, w: jnp.pad(a.reshape(Bs, w).astype(BF16), ((0, MS - Bs), (0, 0)))
        xs = _merge(xs, pad_rows(o_d, 1024), pad_rows(o_m, 1024), pad_rows(o_n, 2048), mg, wp, l)
        outs[1].append(rd[:Bs].reshape(Bs, 1, 2, DIFF_KV_HEADS, HEAD_DIM))
        outs[3].append(rm[:Bs].reshape(Bs, 1, 2, MOBA_KV_HEADS, HEAD_DIM))
        outs[5].append(rn[:Bs].reshape(Bs, 1, 4, NSA_KV_HEADS, HEAD_DIM))
        new_win = win[:Bs].reshape(Bs, 1, 2, NSA_KV_HEADS, HEAD_DIM)
        outs[7].append(jnp.concatenate([state_nsa_win[l], new_win], axis=1)[:, 1:])

        xp = _ffn(xp, ffn2_norm[l], ffn2_w_gu, wp['down2'], l)
        xs = _ffn(xs, ffn2_norm[l], ffn2_w_gu, wp['down2'], l)

    return (xp.reshape(B, T, D), xs[:Bs].reshape(Bs, 1, D)) + tuple(jnp.stack(o) for o in outs)
```

```python
import functools
import math

import jax
import jax.numpy as jnp
import numpy as np
from jax import lax
from jax.experimental import pallas as pl
from jax.experimental.pallas import tpu as pltpu

F32 = jnp.float32
BF16 = jnp.bfloat16
I32 = jnp.int32

D_MODEL = 4096
D_FF = 11008
HEAD_DIM = 128
PAGE_SIZE = 128
ROPE_THETA = 10000.0
EPS = 1e-6
NEG = -1e30
TINY = 1e-30
DIFF_HEADS, DIFF_KV_HEADS, DIFF_DH = 8, 4, 64
MOBA_HEADS, MOBA_KV_HEADS, MOBA_BLOCK, MOBA_TOPK = 8, 4, 256, 3
NSA_HEADS, NSA_KV_HEADS = 16, 2
NSA_REP = NSA_HEADS // NSA_KV_HEADS
CMP_BLOCK, CMP_STRIDE, CMP_HIDDEN = 32, 16, 256
SEL_BLOCK, SEL_TOPK, WINDOW = 64, 16, 512
FORCE_SCORE = 1e4
QKV_COLS = 7680
NG_COLS = NSA_HEADS * 3
NG_PAD = NSA_KV_HEADS * 128
MG_COLS = 3 * D_MODEL

LANES = 128
SUBLANES = 8
VMEM_LIMIT_BYTES = 56 * 1024 * 1024
SCALE128 = HEAD_DIM ** -0.5
SLOTS = 8


def _cparams(*sem):
    return pltpu.CompilerParams(dimension_semantics=sem, vmem_limit_bytes=VMEM_LIMIT_BYTES)


def _dot(a, b):
    return jnp.dot(a, b, preferred_element_type=F32)


def _dot_nt(a, b):
    return lax.dot_general(a, b, (((1,), (1,)), ((), ())), preferred_element_type=F32)


def _dot_exact(a, b):
    return jnp.dot(a, b, preferred_element_type=F32, precision=lax.Precision.HIGHEST)


def _rmsnorm_kernel(x_ref, g_ref, o_ref):
    x = x_ref[...]
    ms = jnp.mean(x * x, axis=-1, keepdims=True)
    o_ref[...] = (x * lax.rsqrt(ms + EPS) * g_ref[...]).astype(o_ref.dtype)


def rmsnorm_bf16(x, g):
    M, D = x.shape
    tm = min(M, 256)
    return pl.pallas_call(
        _rmsnorm_kernel, grid=(M // tm,),
        in_specs=[pl.BlockSpec((tm, D), lambda i: (i, 0)), pl.BlockSpec((1, D), lambda i: (0, 0))],
        out_specs=pl.BlockSpec((tm, D), lambda i: (i, 0)),
        out_shape=jax.ShapeDtypeStruct((M, D), BF16),
        compiler_params=_cparams("parallel"), name="rmsnorm",
    )(x, g.reshape(1, D))


def _mm_kernel(a_ref, b_ref, o_ref, *, sigmoid):
    acc = _dot(a_ref[...], b_ref[...].astype(BF16))
    if sigmoid:
        acc = jax.nn.sigmoid(acc)
    o_ref[...] = acc.astype(o_ref.dtype)


def matmul(a, w, layer, *, n_cols, tn, out_dtype=F32, sigmoid=False, name="matmul"):
    M, K = a.shape
    tm = min(M, 1024)
    return pl.pallas_call(
        functools.partial(_mm_kernel, sigmoid=sigmoid), grid=(M // tm, n_cols // tn),
        in_specs=[pl.BlockSpec((tm, K), lambda i, j: (i, 0)),
                  pl.BlockSpec((None, K, tn), lambda i, j: (layer, 0, j))],
        out_specs=pl.BlockSpec((tm, tn), lambda i, j: (i, j)),
        out_shape=jax.ShapeDtypeStruct((M, n_cols), out_dtype),
        compiler_params=_cparams("parallel", "parallel"), name=name,
    )(a, w)


def _mm_swiglu_kernel(a_ref, bg_ref, bu_ref, o_ref):
    a = a_ref[...]
    g = _dot(a, bg_ref[...].astype(BF16))
    u = _dot(a, bu_ref[...].astype(BF16))
    o_ref[...] = (g * jax.nn.sigmoid(g) * u).astype(o_ref.dtype)


def matmul_swiglu(a, w_gu, layer, *, tn=256):
    M, K = a.shape
    F = w_gu.shape[2] // 2
    tm = min(M, 1024)
    nf = F // tn
    return pl.pallas_call(
        _mm_swiglu_kernel, grid=(M // tm, nf),
        in_specs=[pl.BlockSpec((tm, K), lambda i, j: (i, 0)),
                  pl.BlockSpec((None, K, tn), lambda i, j: (layer, 0, j)),
                  pl.BlockSpec((None, K, tn), lambda i, j: (layer, 0, j + nf))],
        out_specs=pl.BlockSpec((tm, tn), lambda i, j: (i, j)),
        out_shape=jax.ShapeDtypeStruct((M, F), BF16),
        compiler_params=_cparams("parallel", "parallel"), name="ffn_gate_up",
    )(a, w_gu, w_gu)


def _mm_res_kernel(a_ref, b_ref, x_ref, o_ref, *, alpha):
    o_ref[...] = x_ref[...] + alpha * _dot(a_ref[...], b_ref[...].astype(BF16))


def matmul_residual(a, w, layer, x, alpha, *, tm, tn, name):
    M, K = a.shape
    N = w.shape[2]
    tm = min(M, tm)
    return pl.pallas_call(
        functools.partial(_mm_res_kernel, alpha=alpha), grid=(M // tm, N // tn),
        in_specs=[pl.BlockSpec((tm, K), lambda i, j: (i, 0)),
                  pl.BlockSpec((None, K, tn), lambda i, j: (layer, 0, j)),
                  pl.BlockSpec((tm, tn), lambda i, j: (i, j))],
        out_specs=pl.BlockSpec((tm, tn), lambda i, j: (i, j)),
        out_shape=jax.ShapeDtypeStruct((M, N), F32),
        compiler_params=_cparams("parallel", "parallel"), name=name,
    )(a, w, x)


def _mm_merge_kernel(od_ref, om_ref, on_ref, wd_ref, wm_ref, wn_ref, g0_ref, g1_ref, g2_ref, o_ref):
    u = g0_ref[...] * _dot(od_ref[...], wd_ref[...])
    u = u + g1_ref[...] * _dot(om_ref[...], wm_ref[...])
    u = u + g2_ref[...] * _dot(on_ref[...], wn_ref[...])
    o_ref[...] = u.astype(o_ref.dtype)


def matmul_merge(o_d, o_m, o_n, w_branch, layer, mg, *, tn=512):
    M = o_d.shape[0]
    D = w_branch.shape[2]
    tm = min(M, 1024)
    wd, wm, wn = o_d.shape[1], o_m.shape[1], o_n.shape[1]
    nd = D // tn
    return pl.pallas_call(
        _mm_merge_kernel, grid=(M // tm, nd),
        in_specs=[pl.BlockSpec((tm, wd), lambda i, j: (i, 0)),
                  pl.BlockSpec((tm, wm), lambda i, j: (i, 0)),
                  pl.BlockSpec((tm, wn), lambda i, j: (i, 0)),
                  pl.BlockSpec((None, wd, tn), lambda i, j: (layer, 0, j)),
                  pl.BlockSpec((None, wm, tn), lambda i, j: (layer, 1, j)),
                  pl.BlockSpec((None, wn, tn), lambda i, j: (layer, 1, j)),
                  pl.BlockSpec((tm, tn), lambda i, j: (i, j)),
                  pl.BlockSpec((tm, tn), lambda i, j: (i, j + nd)),
                  pl.BlockSpec((tm, tn), lambda i, j: (i, j + 2 * nd))],
        out_specs=pl.BlockSpec((tm, tn), lambda i, j: (i, j)),
        out_shape=jax.ShapeDtypeStruct((M, D), BF16),
        compiler_params=_cparams("parallel", "parallel"), name="merge",
    )(o_d, o_m, o_n, w_branch, w_branch, w_branch, mg, mg, mg)


def _post_kernel(p_ref, c128_ref, s128_ref, c64_ref, s64_ref, gd_ref, gm_ref, gn_ref,
                 dq_ref, rd_ref, rdb_ref, mq_ref, rm_ref, rmb_ref, nq_ref, rn_ref, rnb_ref,
                 win_ref, winb_ref, *rest, tm, emit_chunks):
    lane = lax.broadcasted_iota(I32, (tm, LANES), 1)
    lo = lane < 64
    first_half64 = (lane & 63) < 32
    c128, s128 = c128_ref[...], s128_ref[...]
    c64, s64 = c64_ref[...], s64_ref[...]

    def slab(k):
        return p_ref[:, k * LANES:(k + 1) * LANES]

    def nr128(x, g):
        ms = jnp.mean(x * x, axis=-1, keepdims=True)
        y = x * lax.rsqrt(ms + EPS) * g
        return y * c128 + pltpu.roll(y, 64, 1) * s128

    def nr64(x, g):
        x2 = x * x
        s_lo = jnp.sum(jnp.where(lo, x2, 0.0), axis=-1, keepdims=True)
        s_hi = jnp.sum(jnp.where(lo, 0.0, x2), axis=-1, keepdims=True)
        ms = jnp.where(lo, s_lo, s_hi) * (1.0 / 64.0)
        y = x * lax.rsqrt(ms + EPS) * g
        partner = jnp.where(first_half64, pltpu.roll(y, 96, 1), pltpu.roll(y, 32, 1))
        return y * c64 + partner * s64

    def put(ref_f, ref_b, k, val):
        if ref_f is not None:
            ref_f[:, k * LANES:(k + 1) * LANES] = val
        if ref_b is not None:
            ref_b[:, k * LANES:(k + 1) * LANES] = val.astype(BF16)

    gdq, gdk = gd_ref[0:1, :], gd_ref[1:2, :]
    gmq, gmk = gm_ref[0:1, :], gm_ref[1:2, :]
    for h in range(8):
        put(None, dq_ref, h, nr64(slab(h), gdq) * 0.125)
    for h in range(4):
        put(rd_ref, rdb_ref, h, nr64(slab(8 + h), gdk))
        put(rd_ref, rdb_ref, 4 + h, slab(12 + h))
    for h in range(8):
        put(None, mq_ref, h, nr128(slab(16 + h), gmq) * SCALE128)
    for h in range(4):
        put(rm_ref, rmb_ref, h, nr128(slab(24 + h), gmk))
        put(rm_ref, rmb_ref, 4 + h, slab(28 + h))
    for h in range(16):
        put(None, nq_ref, h, nr128(slab(32 + h), gn_ref[0:1, :]) * SCALE128)
    for g in range(2):
        put(rn_ref, rnb_ref, g, nr128(slab(48 + g), gn_ref[1:2, :]))
        put(rn_ref, rnb_ref, 2 + g, slab(50 + g))
        put(rn_ref, rnb_ref, 4 + g, nr128(slab(52 + g), gn_ref[2:3, :]))
        put(rn_ref, rnb_ref, 6 + g, slab(54 + g))
        put(win_ref, winb_ref, g, nr128(slab(56 + g), gn_ref[3:4, :]))
        put(win_ref, winb_ref, 2 + g, slab(58 + g))
    if emit_chunks:
        xc_ref, slab_ref = rest
        n = tm // CMP_STRIDE
        for cg in range(4):
            slab_ref[...] = rn_ref[:, cg * LANES:(cg + 1) * LANES]
            for j in range(CMP_STRIDE):
                xc_ref[0, cg, :, j * LANES:(j + 1) * LANES] = slab_ref[pl.ds(j, n, stride=CMP_STRIDE), :]


def _rope_tables(pos):
    def tab(d):
        inv = ROPE_THETA ** (-jnp.arange(0, d, 2, dtype=F32) / d)
        ang = pos.astype(F32)[:, None] * inv[None, :]
        cos, sin = jnp.cos(ang), jnp.sin(ang)
        c = jnp.concatenate([cos, cos], axis=-1)
        s = jnp.concatenate([-sin, sin], axis=-1)
        rep = LANES // d
        return jnp.tile(c, (1, rep)), jnp.tile(s, (1, rep))
    c128, s128 = tab(128)
    c64, s64 = tab(64)
    return c128, s128, c64, s64


def post_project(proj, tables, gd, gm, gn, *, seq, batch):
    M = proj.shape[0]
    tm = min(M, 256)
    emit_chunks = seq >= tm
    nt = max(seq // tm, 1)
    row = lambda w: pl.BlockSpec((tm, w), lambda i: (i, 0))
    tab = pl.BlockSpec((tm, LANES), lambda i: (i % nt, 0))
    par = lambda r: pl.BlockSpec((r, LANES), lambda i: (0, 0))
    out_shapes = [
        jax.ShapeDtypeStruct((M, 1024), BF16),
        jax.ShapeDtypeStruct((M, 1024), F32), jax.ShapeDtypeStruct((M, 1024), BF16),
        jax.ShapeDtypeStruct((M, 1024), BF16),
        jax.ShapeDtypeStruct((M, 1024), F32), jax.ShapeDtypeStruct((M, 1024), BF16),
        jax.ShapeDtypeStruct((M, 2048), BF16),
        jax.ShapeDtypeStruct((M, 1024), F32), jax.ShapeDtypeStruct((M, 1024), BF16),
        jax.ShapeDtypeStruct((M, 512), F32), jax.ShapeDtypeStruct((M, 512), BF16),
    ]
    out_specs = [row(1024), row(1024), row(1024), row(1024), row(1024), row(1024), row(2048),
                 row(1024), row(1024), row(512), row(512)]
    if emit_chunks:
        nchunk = seq // CMP_STRIDE
        out_shapes.append(jax.ShapeDtypeStruct((batch, 4, nchunk, CMP_STRIDE * LANES), F32))
        out_specs.append(pl.BlockSpec((1, 4, tm // CMP_STRIDE, CMP_STRIDE * LANES),
                                      lambda i: (i // nt, 0, i % nt, 0)))
    return pl.pallas_call(
        functools.partial(_post_kernel, tm=tm, emit_chunks=emit_chunks), grid=(M // tm,),
        in_specs=[row(QKV_COLS), tab, tab, tab, tab, par(2), par(2), par(4)],
        out_specs=out_specs, out_shape=out_shapes,
        scratch_shapes=[pltpu.VMEM((tm, LANES), F32)] if emit_chunks else [],
        compiler_params=_cparams("parallel"), name="post_project",
    )(proj, *tables, gd, gm, gn)


def _online(carry, s, v, ok=None):
    m, l, acc = carry
    m_new = jnp.maximum(m, jnp.max(s, axis=-1, keepdims=True))
    alpha = jnp.exp(m - m_new)
    p = jnp.exp(s - m_new)
    if ok is not None:
        p = jnp.where(ok, p, 0.0)
    l = alpha * l + jnp.sum(p, axis=-1, keepdims=True)
    acc = alpha * acc + _dot(p.astype(BF16), v)
    return m_new, l, acc


def _init_carry(rows, width):
    return (jnp.full((rows, 1), NEG, F32), jnp.zeros((rows, 1), F32), jnp.zeros((rows, width), F32))


def _online_t(carry, s_t, v, ok=None):
    m, l, acc = carry
    m_new = jnp.maximum(m, jnp.max(s_t, axis=0, keepdims=True))
    alpha = jnp.exp(m - m_new)
    p = jnp.exp(s_t - m_new)
    if ok is not None:
        p = jnp.where(ok, p, 0.0)
    l = alpha * l + jnp.sum(p, axis=0, keepdims=True)
    acc = alpha * acc + _dot(v.astype(F32).T.astype(BF16), p.astype(BF16))
    return m_new, l, acc


def _init_carry_t(rows, width):
    return (jnp.full((1, rows), NEG, F32), jnp.zeros((1, rows), F32), jnp.zeros((width, rows), F32))


def _flash_reset(m_ref, l_ref, acc_ref):
    m_ref[...] = jnp.full(m_ref.shape, NEG, F32)
    l_ref[...] = jnp.zeros(l_ref.shape, F32)
    acc_ref[...] = jnp.zeros(acc_ref.shape, F32)


def _lambda(lam_ref, lam_init):
    lq = lam_ref[...]
    a = jnp.sum(lq[0:1, :] * lq[1:2, :], axis=-1, keepdims=True)
    b = jnp.sum(lq[2:3, :] * lq[3:4, :], axis=-1, keepdims=True)
    return jnp.exp(a) - jnp.exp(b) + lam_init


def _head_norm(o, g, post):
    ms = jnp.mean(o * o, axis=-1, keepdims=True)
    return o * lax.rsqrt(ms + EPS) * g * post


def _diff_prompt_kernel(lam_ref, gn_ref, q_ref, k_ref, v_ref, o_ref, *, tq, lam_init):
    i = pl.program_id(2)
    lane = lax.broadcasted_iota(I32, (tq, LANES), 1)
    q = q_ref[...]
    zero = jnp.zeros((tq, LANES), BF16)
    parts = []
    for r in range(2):
        qr = q[:, r * LANES:(r + 1) * LANES]
        parts.append(jnp.where(lane < 64, qr, zero))
        parts.append(jnp.where(lane < 64, zero, qr))
    Q = jnp.concatenate(parts, axis=0)
    R = 4 * tq

    def blk(kb):
        s0 = pl.multiple_of(kb * tq, tq)
        return k_ref[pl.ds(s0, tq), :], v_ref[pl.ds(s0, tq), :]

    def body(kb, carry):
        k, v = blk(kb)
        return _online_t(carry, _dot_nt(k, Q), v)

    carry = lax.fori_loop(0, i, body, _init_carry_t(R, LANES))
    k, v = blk(i)
    key = lax.broadcasted_iota(I32, (tq, R), 0)
    qrow = lax.broadcasted_iota(I32, (tq, R), 1) & (tq - 1)
    s = jnp.where(key <= qrow, _dot_nt(k, Q), NEG)
    m, l, acc = _online_t(carry, s, v)
    o = acc / jnp.maximum(l, TINY)
    lam = _lambda(lam_ref, lam_init)
    g = gn_ref[...]
    for r in range(2):
        o_r = o[:, (2 * r) * tq:(2 * r + 1) * tq] - lam * o[:, (2 * r + 1) * tq:(2 * r + 2) * tq]
        o_ref[:, r * LANES:(r + 1) * LANES] = _head_norm(o_r.T, g, 1.0 - lam_init).astype(o_ref.dtype)


def diff_prompt(dq, rows_b, lam_param, out_norm, lam_init, *, batch, seq):
    M = dq.shape[0]
    tq = 256
    nq = seq // tq
    return pl.pallas_call(
        functools.partial(_diff_prompt_kernel, tq=tq, lam_init=lam_init),
        grid=(batch, DIFF_KV_HEADS, nq),
        in_specs=[pl.BlockSpec((4, DIFF_DH), lambda b, g, i: (0, 0)),
                  pl.BlockSpec((1, LANES), lambda b, g, i: (0, 0)),
                  pl.BlockSpec((tq, 2 * LANES), lambda b, g, i: (b * nq + i, g)),
                  pl.BlockSpec((seq, LANES), lambda b, g, i: (b, g)),
                  pl.BlockSpec((seq, LANES), lambda b, g, i: (b, DIFF_KV_HEADS + g))],
        out_specs=pl.BlockSpec((tq, 2 * LANES), lambda b, g, i: (b * nq + i, g)),
        out_shape=jax.ShapeDtypeStruct((M, DIFF_HEADS * LANES), BF16),
        compiler_params=_cparams("parallel", "parallel", "parallel"), name="diff_prompt",
    )(lam_param, out_norm.reshape(1, LANES), dq, rows_b, rows_b)


def _rank_lt(vals, n_cand, lane):
    cnt = jnp.zeros(vals.shape, F32)
    for m in range(n_cand):
        col = vals[:, m:m + 1]
        beats = (col > vals) | ((col == vals) & (lane > m))
        cnt = cnt + jnp.where(beats, 1.0, 0.0)
    return cnt


def _rank_lt_t(vals, n_cand, idx):
    cnt = jnp.zeros(vals.shape, F32)
    for m in range(n_cand):
        row = vals[m:m + 1, :]
        beats = (row > vals) | ((row == vals) & (idx > m))
        cnt = cnt + jnp.where(beats, 1.0, 0.0)
    return cnt


def _moba_prompt_kernel(q_ref, kf_ref, k_ref, v_ref, o_ref, *, tq, nb):
    i = pl.program_id(2)
    q = q_ref[...]
    Q = jnp.concatenate([q[:, :LANES], q[:, LANES:]], axis=0)
    R = 2 * tq
    nbp = -(-nb // SUBLANES) * SUBLANES
    blkid = lax.broadcasted_iota(I32, (nbp, R), 0)
    kms = [jnp.mean(kf_ref[n * MOBA_BLOCK:(n + 1) * MOBA_BLOCK, :], axis=0, keepdims=True) for n in range(nb)]
    kmean = jnp.concatenate(kms + [jnp.zeros((LANES - nb, LANES), F32)], axis=0)
    gate = _dot_nt(kmean.astype(BF16), Q)[0:nbp]
    gate = jnp.where(blkid < i, gate, NEG)
    cnt = _rank_lt_t(gate, nb, blkid)
    sel = jnp.where((cnt < MOBA_TOPK) & (gate > NEG / 2), 1.0, 0.0)

    def blk(kb):
        s0 = pl.multiple_of(kb * tq, tq)
        return k_ref[pl.ds(s0, tq), :], v_ref[pl.ds(s0, tq), :]

    def body(kb, carry):
        k, v = blk(kb)
        on = jnp.sum(jnp.where(blkid == kb, sel, 0.0), axis=0, keepdims=True) > 0.5
        s = jnp.where(on, _dot_nt(k, Q), NEG)
        return _online_t(carry, s, v, ok=on)

    carry = lax.fori_loop(0, i, body, _init_carry_t(R, LANES))
    k, v = blk(i)
    key = lax.broadcasted_iota(I32, (tq, R), 0)
    qrow = lax.broadcasted_iota(I32, (tq, R), 1) & (tq - 1)
    s = jnp.where(key <= qrow, _dot_nt(k, Q), NEG)
    m, l, acc = _online_t(carry, s, v)
    o = acc / jnp.maximum(l, TINY)
    for r in range(2):
        o_ref[:, r * LANES:(r + 1) * LANES] = o[:, r * tq:(r + 1) * tq].T.astype(o_ref.dtype)


def moba_prompt(mq, rows_f, rows_b, *, batch, seq):
    M = mq.shape[0]
    tq = MOBA_BLOCK
    nq = seq // tq
    return pl.pallas_call(
        functools.partial(_moba_prompt_kernel, tq=tq, nb=seq // MOBA_BLOCK),
        grid=(batch, MOBA_KV_HEADS, nq),
        in_specs=[pl.BlockSpec((tq, 2 * LANES), lambda b, g, i: (b * nq + i, g)),
                  pl.BlockSpec((seq, LANES), lambda b, g, i: (b, g)),
                  pl.BlockSpec((seq, LANES), lambda b, g, i: (b, g)),
                  pl.BlockSpec((seq, LANES), lambda b, g, i: (b, MOBA_KV_HEADS + g))],
        out_specs=pl.BlockSpec((tq, 2 * LANES), lambda b, g, i: (b * nq + i, g)),
        out_shape=jax.ShapeDtypeStruct((M, MOBA_HEADS * LANES), BF16),
        compiler_params=_cparams("parallel", "parallel", "parallel"), name="moba_prompt",
    )(mq, rows_f, rows_b, rows_b)


def _compress_kernel(x_ref, pe_ref, w0_ref, w1_ref, w2_ref, o_ref, *, n):
    x = x_ref[0, 0]
    y0 = _dot((x + pe_ref[0, 0:1, :]).astype(BF16), w0_ref[0])
    y1 = _dot((x + pe_ref[0, 1:2, :]).astype(BF16), w1_ref[0])
    hid = y0 + pltpu.roll(y1, n - 1, 0)
    act = hid * jax.nn.sigmoid(hid)
    o_ref[0, 0] = _dot(act.astype(BF16), w2_ref[0]).astype(o_ref.dtype)


def compress(xc, pe, w0, w1, w2):
    B, _, n, W = xc.shape
    return pl.pallas_call(
        functools.partial(_compress_kernel, n=n), grid=(B, 4),
        in_specs=[pl.BlockSpec((1, 1, n, W), lambda b, c: (b, c, 0, 0)),
                  pl.BlockSpec((1, 2, W), lambda b, c: (c // 2, 0, 0)),
                  pl.BlockSpec((1, W, CMP_HIDDEN), lambda b, c: (c // 2, 0, 0)),
                  pl.BlockSpec((1, W, CMP_HIDDEN), lambda b, c: (c // 2, 0, 0)),
                  pl.BlockSpec((1, CMP_HIDDEN, LANES), lambda b, c: (c // 2, 0, 0))],
        out_specs=pl.BlockSpec((1, 1, n, LANES), lambda b, c: (b, c, 0, 0)),
        out_shape=jax.ShapeDtypeStruct((B, 4, n, LANES), BF16),
        compiler_params=_cparams("parallel", "parallel"), name="compress",
    )(xc, pe, w0, w1, w2)


def _nsa_prompt_kernel(q_ref, g_ref, ck_ref, cv_ref, sk_ref, sv_ref, wk_ref, wv_ref, imp_ref, exp_ref,
                       o_ref, msel_ref, *, tq, seq):
    i = pl.program_id(2)
    H = NSA_REP
    R = H * tq
    n_cmp = seq // CMP_STRIDE - 1
    n_sel = seq // SEL_BLOCK
    q = q_ref[...]
    Q = jnp.concatenate([q[:, r * LANES:(r + 1) * LANES] for r in range(H)], axis=0)
    row = lax.broadcasted_iota(I32, (tq, R), 0)
    qp = i * tq + (lax.broadcasted_iota(I32, (tq, R), 1) & (tq - 1))
    row_q = lax.broadcasted_iota(I32, (tq, tq), 0)
    qp_q = i * tq + lax.broadcasted_iota(I32, (tq, tq), 1)

    s = _dot_nt(ck_ref[0, 0], Q)
    ok = (row < n_cmp) & (row * CMP_STRIDE + (CMP_BLOCK - 1) <= qp)
    s = jnp.where(ok, s, NEG)
    p = jnp.where(ok, jnp.exp(s - jnp.max(s, axis=0, keepdims=True)), 0.0)
    pc = p / jnp.maximum(jnp.sum(p, axis=0, keepdims=True), TINY)
    o_c = _dot(cv_ref[0, 0].astype(F32).T.astype(BF16), pc.astype(BF16))

    pcs = pc[:, 0:tq]
    for r in range(1, H):
        pcs = pcs + pc[:, r * tq:(r + 1) * tq]
    imp = _dot_exact(imp_ref[...], pcs)[0:n_sel]
    blk_id = lax.broadcasted_iota(I32, (n_sel, tq), 0)
    cur = (i * tq + lax.broadcasted_iota(I32, (n_sel, tq), 1)) >> 6
    valid = blk_id <= cur
    forced = (blk_id == 0) | (blk_id == cur) | (blk_id == cur - 1)
    imp = jnp.where(valid, jnp.where(forced, FORCE_SCORE, imp), NEG)
    cnt = _rank_lt_t(imp, n_sel, blk_id)
    sel = jnp.where((cnt < SEL_TOPK) & (imp > NEG / 2), 1.0, 0.0)
    sel = jnp.concatenate([sel, jnp.zeros((LANES - n_sel, tq), F32)], axis=0)
    msel_ref[...] = _dot(exp_ref[...], sel.astype(BF16))

    def blk(kref, vref, kb):
        s0 = pl.multiple_of(kb * tq, tq)
        return kref[pl.ds(s0, tq), :], vref[pl.ds(s0, tq), :]

    def attend(carry, k, v, ok_q):
        bias = jnp.concatenate([jnp.where(ok_q, 0.0, NEG)] * H, axis=1)
        return _online_t(carry, _dot_nt(k, Q) + bias, v)

    def sel_body(kb, carry):
        k, v = blk(sk_ref, sv_ref, kb)
        on = msel_ref[pl.ds(pl.multiple_of(kb * tq, tq), tq), :] > 0.5
        return attend(carry, k, v, on & (kb * tq + row_q <= qp_q))

    m, l, acc = lax.fori_loop(0, i + 1, sel_body, _init_carry_t(R, LANES))
    o_s = acc / jnp.maximum(l, TINY)

    def win_body(kb, carry):
        k, v = blk(wk_ref, wv_ref, kb)
        dist = qp_q - (kb * tq + row_q)
        return attend(carry, k, v, (dist >= 0) & (dist <= WINDOW))

    m, l, acc = lax.fori_loop(jnp.maximum(i - WINDOW // tq, 0), i + 1, win_body, _init_carry_t(R, LANES))
    o_w = acc / jnp.maximum(l, TINY)

    gates = g_ref[...].T
    for r in range(H):
        cs = slice(r * tq, (r + 1) * tq)
        o = (gates[r:r + 1] * o_c[:, cs] + gates[H + r:H + r + 1] * o_s[:, cs]
             + gates[2 * H + r:2 * H + r + 1] * o_w[:, cs])
        o_ref[:, r * LANES:(r + 1) * LANES] = o.T.astype(o_ref.dtype)


def _importance_matrix(n_tok_pad, n_tok, n_blk_pad, n_blk):
    r_ = SEL_BLOCK // CMP_STRIDE
    f_ = CMP_BLOCK // CMP_STRIDE - 1
    i = np.arange(n_tok_pad)[:, None]
    j = np.arange(n_blk_pad)[None, :]
    a = (i >= r_ * j - f_) & (i <= r_ * j + r_ - 1) & (i < n_tok) & (j < n_blk)
    return jnp.asarray(a.astype(np.float32))


def _expand_matrix(n_blk_pad, n_keys):
    j = np.arange(n_blk_pad)[:, None]
    t = np.arange(n_keys)[None, :]
    return jnp.asarray((t // SEL_BLOCK == j).astype(np.float32), dtype=BF16)


def nsa_prompt(nq, ng, cmp_tok, rows_b, win_b, *, batch, seq):
    M = nq.shape[0]
    tq = 128
    nt = seq // tq
    G = NSA_KV_HEADS
    n_tok = seq // CMP_STRIDE
    assert n_tok == LANES and tq == LANES and (seq // SEL_BLOCK) % SUBLANES == 0
    imp_m = _importance_matrix(LANES, n_tok - 1, LANES, seq // SEL_BLOCK).T
    exp_m = _expand_matrix(LANES, seq).T
    kv = lambda c: pl.BlockSpec((seq, LANES), lambda b, g, i: (b, c + g))
    return pl.pallas_call(
        functools.partial(_nsa_prompt_kernel, tq=tq, seq=seq), grid=(batch, G, nt),
        in_specs=[pl.BlockSpec((tq, NSA_REP * LANES), lambda b, g, i: (b * nt + i, g)),
                  pl.BlockSpec((tq, LANES), lambda b, g, i: (b * nt + i, g)),
                  pl.BlockSpec((1, 1, n_tok, LANES), lambda b, g, i: (b, g, 0, 0)),
                  pl.BlockSpec((1, 1, n_tok, LANES), lambda b, g, i: (b, G + g, 0, 0)),
                  kv(4), kv(6), kv(0), kv(2),
                  pl.BlockSpec((LANES, LANES), lambda b, g, i: (0, 0)),
                  pl.BlockSpec((seq, LANES), lambda b, g, i: (0, 0))],
        out_specs=pl.BlockSpec((tq, NSA_REP * LANES), lambda b, g, i: (b * nt + i, g)),
        out_shape=jax.ShapeDtypeStruct((M, NSA_HEADS * LANES), BF16),
        scratch_shapes=[pltpu.VMEM((seq, tq), F32)],
        compiler_params=_cparams("parallel", "parallel", "arbitrary"), name="nsa_prompt",
    )(nq, ng, cmp_tok, cmp_tok, rows_b, rows_b, win_b, win_b, imp_m, exp_m)


def _slot_scores(q, raw_bf16, key_slot, nslots):
    s = _dot_nt(q, raw_bf16)
    col = lax.broadcasted_iota(I32, s.shape, 1) & (nslots - 1)
    return s, col == key_slot


def _slot_update(carry, s, ok, raw_bf16, shift):
    m_old, l, acc = carry
    m_new = jnp.maximum(m_old, jnp.max(s, axis=-1, keepdims=True))
    alpha = jnp.exp(m_old - m_new)
    p = jnp.where(ok, jnp.exp(s - m_new), 0.0)
    l = alpha * l + jnp.sum(p, axis=-1, keepdims=True)
    acc = alpha * acc + _dot(pltpu.roll(p, shift, 1).astype(BF16), raw_bf16)
    return m_new, l, acc


def _with_new(m, l, acc, s_new, vn, on=True):
    s_new = jnp.where(on, s_new, NEG)
    m2 = jnp.maximum(m, s_new)
    a2 = jnp.exp(m - m2)
    pn = jnp.where(on, jnp.exp(s_new - m2), 0.0)
    return (a2 * acc + pn * vn) / jnp.maximum(a2 * l + pn, TINY)


def _diff_sample_kernel(pt_ref, lam_ref, gn_ref, q_ref, kn_ref, vn_ref, *refs, P, lam_init):
    pages = refs[:P]
    o_ref = refs[P]
    m_ref, l_ref, acc_ref = refs[P + 1:]
    j = pl.program_id(1)

    @pl.when(j == 0)
    def _():
        _flash_reset(m_ref, l_ref, acc_ref)

    Q = q_ref[0]
    raw = jnp.concatenate([pages[p][0, 0].astype(BF16) for p in range(P)], axis=0)
    s = _dot_nt(Q, raw)
    slot = lax.broadcasted_iota(I32, s.shape, 1) & (SLOTS - 1)
    grp = lax.broadcasted_iota(I32, s.shape, 0) >> 2
    ok = slot == grp
    m_ref[...], l_ref[...], acc_ref[...] = _slot_update(
        (m_ref[...], l_ref[...], acc_ref[...]), jnp.where(ok, s, NEG), ok, raw, DIFF_KV_HEADS)

    @pl.when(j == pl.num_programs(1) - 1)
    def _():
        s_new = jnp.sum(Q.astype(F32) * kn_ref[0].astype(F32), axis=-1, keepdims=True)
        a = _with_new(m_ref[...], l_ref[...], acc_ref[...], s_new, vn_ref[0].astype(F32))
        lam = _lambda(lam_ref, lam_init)
        g = gn_ref[...]
        outs = []
        for h in range(DIFF_HEADS):
            o = a[2 * h:2 * h + 1] - lam * a[2 * h + 1:2 * h + 2]
            outs.append(_head_norm(o, g, 1.0 - lam_init))
        o_ref[0] = jnp.concatenate(outs, axis=0).astype(o_ref.dtype)


def diff_sample(q16, kn16, vn16, cache, page_table_flat, lam_param, out_norm, lam_init, *, layer, batch, n_pages):
    P = 8
    R = PAGE_SIZE * SLOTS
    page = lambda p: pl.BlockSpec((1, 1, R, LANES), lambda b, j, pt: (layer, pt[b * n_pages + j * P + p], 0, 0))
    per_b = pl.BlockSpec((1, 16, LANES), lambda b, j, pt: (b, 0, 0))
    gs = pltpu.PrefetchScalarGridSpec(
        num_scalar_prefetch=1, grid=(batch, n_pages // P),
        in_specs=[pl.BlockSpec((4, DIFF_DH), lambda b, j, pt: (0, 0)),
                  pl.BlockSpec((1, LANES), lambda b, j, pt: (0, 0)),
                  per_b, per_b, per_b] + [page(p) for p in range(P)],
        out_specs=pl.BlockSpec((1, DIFF_HEADS, LANES), lambda b, j, pt: (b, 0, 0)),
        scratch_shapes=[pltpu.VMEM((16, 1), F32), pltpu.VMEM((16, 1), F32), pltpu.VMEM((16, LANES), F32)])
    return pl.pallas_call(
        functools.partial(_diff_sample_kernel, P=P, lam_init=lam_init), grid_spec=gs,
        out_shape=jax.ShapeDtypeStruct((batch, DIFF_HEADS, LANES), BF16),
        compiler_params=_cparams("parallel", "arbitrary"), name="diff_sample",
    )(page_table_flat, lam_param, out_norm.reshape(1, LANES), q16, kn16, vn16, *([cache] * P))


def _moba_gate_kernel(pt_ref, q_ref, *refs, nblk, bps):
    pages = refs[:2 * bps]
    o_ref, gate_ref = refs[2 * bps:]
    n = pl.program_id(1)
    lane = lax.broadcasted_iota(I32, (MOBA_HEADS, LANES), 1)

    @pl.when(n == 0)
    def _():
        gate_ref[...] = jnp.full(gate_ref.shape, NEG, F32)

    q = q_ref[0].astype(F32)
    gate = gate_ref[...]
    for t in range(bps):
        ksum = (jnp.sum(pages[2 * t][0, 0].reshape(PAGE_SIZE, SLOTS, LANES), axis=0)
                + jnp.sum(pages[2 * t + 1][0, 0].reshape(PAGE_SIZE, SLOTS, LANES), axis=0))
        kmean = (ksum * (1.0 / MOBA_BLOCK)).astype(BF16).astype(F32)
        per_head = jnp.concatenate([kmean[h // 2:h // 2 + 1] for h in range(MOBA_HEADS)], axis=0)
        g = jnp.sum(q * per_head, axis=-1, keepdims=True)
        gate = jnp.where(lane == n * bps + t, g, gate)
    gate_ref[...] = gate

    @pl.when(n == nblk // bps - 1)
    def _():
        cnt = _rank_lt(gate, nblk, lane)
        lanef = lane.astype(F32)
        out = jnp.zeros(gate.shape, F32)
        for k in range(MOBA_TOPK):
            hit = cnt == float(k)
            idx = jnp.sum(jnp.where(hit, lanef, 0.0), axis=-1, keepdims=True)
            val = jnp.max(jnp.where(hit, gate, NEG), axis=-1, keepdims=True)
            out = out + jnp.where(lane == k, idx, 0.0) + jnp.where((lane == 4 + k) & (val > NEG / 2), 1.0, 0.0)
        o_ref[0] = out.astype(I32)


def moba_gate_sample(q, cache, page_table_flat, *, layer, batch, n_pages):
    nblk = n_pages * PAGE_SIZE // MOBA_BLOCK
    ppb = MOBA_BLOCK // PAGE_SIZE
    bps = 4 if nblk % 4 == 0 else 1
    assert ppb == 2 and nblk <= LANES
    R = PAGE_SIZE * SLOTS
    page = lambda p: pl.BlockSpec((1, 1, R, LANES),
                                  lambda b, n, pt: (layer, pt[b * n_pages + n * bps * ppb + p], 0, 0))
    gs = pltpu.PrefetchScalarGridSpec(
        num_scalar_prefetch=1, grid=(batch, nblk // bps),
        in_specs=[pl.BlockSpec((1, MOBA_HEADS, LANES), lambda b, n, pt: (b, 0, 0))]
        + [page(p) for p in range(bps * ppb)],
        out_specs=pl.BlockSpec((1, MOBA_HEADS, LANES), lambda b, n, pt: (b, 0, 0)),
        scratch_shapes=[pltpu.VMEM((MOBA_HEADS, LANES), F32)])
    return pl.pallas_call(
        functools.partial(_moba_gate_kernel, nblk=nblk, bps=bps), grid_spec=gs,
        out_shape=jax.ShapeDtypeStruct((batch, MOBA_HEADS, LANES), I32),
        compiler_params=_cparams("parallel", "arbitrary"), name="moba_gate_sample",
    )(page_table_flat, q, *([cache] * (bps * ppb)))


def _moba_sample_kernel(pg_ref, ok_ref, q_ref, kn_ref, vn_ref, *refs, npage):
    pages, o_ref = refs[:npage], refs[npage]
    b, h = pl.program_id(0), pl.program_id(1)
    Q = q_ref[0, 0]
    carry = _init_carry(8, LANES)
    for t in range(npage):
        raw = pages[t][0, 0].astype(BF16)
        on = ok_ref[(b * MOBA_HEADS + h) * MOBA_TOPK + t // (npage // MOBA_TOPK)] > 0
        s, is_key = _slot_scores(Q, raw, h // 2, SLOTS)
        ok = is_key & on
        carry = _slot_update(carry, jnp.where(ok, s, NEG), ok, raw, MOBA_KV_HEADS)
    s_new = jnp.sum(Q.astype(F32) * kn_ref[0].astype(F32), axis=-1, keepdims=True)
    o = _with_new(*carry, s_new, vn_ref[0].astype(F32))
    o_ref[0] = o[0:1].astype(o_ref.dtype)


def moba_sample(q8, kv_new, cache, sel_pages, sel_ok, *, layer, batch):
    npage = MOBA_TOPK * (MOBA_BLOCK // PAGE_SIZE)
    G = MOBA_KV_HEADS
    R = PAGE_SIZE * SLOTS
    page = lambda t: pl.BlockSpec(
        (1, 1, R, LANES), lambda b, h, pgs, oks: (layer, pgs[(b * MOBA_HEADS + h) * npage + t], 0, 0))
    gs = pltpu.PrefetchScalarGridSpec(
        num_scalar_prefetch=2, grid=(batch, MOBA_HEADS),
        in_specs=[pl.BlockSpec((1, 1, 8, LANES), lambda b, h, pgs, oks: (b, h, 0, 0)),
                  pl.BlockSpec((1, 1, LANES), lambda b, h, pgs, oks: (b, 0, h // 2)),
                  pl.BlockSpec((1, 1, LANES), lambda b, h, pgs, oks: (b, 0, G + h // 2))]
        + [page(t) for t in range(npage)],
        out_specs=pl.BlockSpec((1, 1, LANES), lambda b, h, pgs, oks: (b, 0, h)))
    return pl.pallas_call(
        functools.partial(_moba_sample_kernel, npage=npage), grid_spec=gs,
        out_shape=jax.ShapeDtypeStruct((batch, 1, MOBA_HEADS * LANES), BF16),
        compiler_params=_cparams("parallel", "parallel"), name="moba_sample",
    )(sel_pages, sel_ok, q8, kv_new, kv_new, *([cache] * npage))


def _nsa_chunk_kernel(pt_ref, *refs, P):
    pages, o_ref = refs[:P], refs[P]
    n = PAGE_SIZE // CMP_STRIDE
    for p in range(P):
        for cg in range(4):
            for j in range(CMP_STRIDE):
                o_ref[0, cg, p * n:(p + 1) * n, j * LANES:(j + 1) * LANES] = (
                    pages[p][0, 0, pl.ds(j * SLOTS + cg, n, stride=CMP_STRIDE * SLOTS), :])


def nsa_chunk_sample(cache, page_table_flat, *, layer, batch, n_pages):
    n = PAGE_SIZE // CMP_STRIDE
    P = 4
    R = PAGE_SIZE * SLOTS
    page = lambda p: pl.BlockSpec((1, 1, R, LANES), lambda b, s, pt: (layer, pt[b * n_pages + s * P + p], 0, 0))
    gs = pltpu.PrefetchScalarGridSpec(
        num_scalar_prefetch=1, grid=(batch, n_pages // P),
        in_specs=[page(p) for p in range(P)],
        out_specs=pl.BlockSpec((1, 4, P * n, CMP_STRIDE * LANES), lambda b, s, pt: (b, 0, s, 0)))
    return pl.pallas_call(
        functools.partial(_nsa_chunk_kernel, P=P), grid_spec=gs,
        out_shape=jax.ShapeDtypeStruct((batch, 4, n_pages * n, CMP_STRIDE * LANES), F32),
        compiler_params=_cparams("parallel", "parallel"), name="nsa_chunk_sample",
    )(page_table_flat, *([cache] * P))


def _nsa_cmp_sample_kernel(q_ref, ck_ref, cv_ref, imp_ref, oc_ref, sel_ref, *, n_tok, n_valid, cur, n_blk_pad):
    Q = q_ref[0, 0]
    lane = lax.broadcasted_iota(I32, (NSA_REP, n_tok), 1)
    ok = lane < n_valid
    s = jnp.where(ok, _dot_nt(Q, ck_ref[0, 0]), NEG)
    p = jnp.where(ok, jnp.exp(s - jnp.max(s, axis=-1, keepdims=True)), 0.0)
    pc = p / jnp.maximum(jnp.sum(p, axis=-1, keepdims=True), TINY)
    oc_ref[0, 0] = _dot(pc.astype(BF16), cv_ref[0, 0])
    pcs = jnp.broadcast_to(jnp.sum(pc, axis=0, keepdims=True), (8, n_tok))
    imp = _dot_exact(pcs, imp_ref[...])[0:1]
    blk = lax.broadcasted_iota(I32, (1, n_blk_pad), 1)
    valid = blk <= cur
    forced = (blk == 0) | (blk == cur) | (blk == cur - 1)
    imp = jnp.where(valid, jnp.where(forced, FORCE_SCORE, imp), NEG)
    rows = jnp.broadcast_to(imp, (n_blk_pad, n_blk_pad))
    cols = rows.T
    mi = lax.broadcasted_iota(I32, (n_blk_pad, n_blk_pad), 0)
    ji = lax.broadcasted_iota(I32, (n_blk_pad, n_blk_pad), 1)
    beats = (cols > rows) | ((cols == rows) & (mi < ji))
    cnt = jnp.sum(jnp.where(beats, 1.0, 0.0), axis=0, keepdims=True)
    blkf = blk.astype(F32)
    lane_o = lax.broadcasted_iota(I32, (1, LANES), 1)
    out = jnp.zeros((1, LANES), F32)
    for k in range(SEL_TOPK):
        hit = cnt == float(k)
        idx = jnp.sum(jnp.where(hit, blkf, 0.0), axis=-1, keepdims=True)
        val = jnp.max(jnp.where(hit, imp, NEG), axis=-1, keepdims=True)
        out = out + jnp.where(lane_o == k, idx, 0.0) + jnp.where((lane_o == SEL_TOPK + k) & (val > NEG / 2), 1.0, 0.0)
    sel_ref[0, 0] = jnp.broadcast_to(out, (8, LANES)).astype(I32)


def nsa_cmp_sample(q8, cmp_tok, *, batch, past_len):
    G = NSA_KV_HEADS
    n_tok = cmp_tok.shape[2]
    n_valid = n_tok - 1
    cur = past_len // SEL_BLOCK
    n_blk = cur + 1
    n_blk_pad = -(-n_blk // LANES) * LANES
    imp_m = _importance_matrix(n_tok, n_tok, n_blk_pad, n_blk)
    return pl.pallas_call(
        functools.partial(_nsa_cmp_sample_kernel, n_tok=n_tok, n_valid=n_valid, cur=cur, n_blk_pad=n_blk_pad),
        grid=(batch, G),
        in_specs=[pl.BlockSpec((1, 1, NSA_REP, LANES), lambda b, g: (b, g, 0, 0)),
                  pl.BlockSpec((1, 1, n_tok, LANES), lambda b, g: (b, g, 0, 0)),
                  pl.BlockSpec((1, 1, n_tok, LANES), lambda b, g: (b, G + g, 0, 0)),
                  pl.BlockSpec((n_tok, n_blk_pad), lambda b, g: (0, 0))],
        out_specs=[pl.BlockSpec((1, 1, NSA_REP, LANES), lambda b, g: (b, g, 0, 0)),
                   pl.BlockSpec((1, 1, 8, LANES), lambda b, g: (b, g, 0, 0))],
        out_shape=[jax.ShapeDtypeStruct((batch, G, NSA_REP, LANES), F32),
                   jax.ShapeDtypeStruct((batch, G, 8, LANES), I32)],
        compiler_params=_cparams("parallel", "parallel"), name="nsa_cmp_sample",
    )(q8, cmp_tok, cmp_tok, imp_m)


def _nsa_sel_sample_kernel(pg_ref, hb_ref, ok_ref, own_ref, q_ref, gt_ref, oc_ref, kn_ref, vn_ref, wkn_ref,
                           wvn_ref, win_ref, *refs):
    blks, o_ref = refs[:SEL_TOPK], refs[SEL_TOPK]
    b, g = pl.program_id(0), pl.program_id(1)
    G = NSA_KV_HEADS
    Q = q_ref[0, 0]
    carry = _init_carry(NSA_REP, LANES)
    for t in range(SEL_TOPK):
        raw = blks[t][0, 0].astype(BF16)
        on = ok_ref[(b * G + g) * SEL_TOPK + t] > 0
        s, is_key = _slot_scores(Q, raw, 2 * G + g, SLOTS)
        ok = is_key & on
        carry = _slot_update(carry, jnp.where(ok, s, NEG), ok, raw, G)
    Qf = Q.astype(F32)

    def new_score(kn_ref):
        return jnp.sum(Qf * kn_ref[0].astype(F32), axis=-1, keepdims=True)

    own = own_ref[b * G + g] > 0
    o_s = _with_new(*carry, new_score(kn_ref), vn_ref[0].astype(F32), own)
    raww = win_ref[0, 0].astype(BF16)
    sw, okw = _slot_scores(Q, raww, g, 2 * G)
    cw = _slot_update(_init_carry(NSA_REP, LANES), jnp.where(okw, sw, NEG), okw, raww, G)
    o_w = _with_new(*cw, new_score(wkn_ref), wvn_ref[0].astype(F32))
    gt = gt_ref[0, 0]
    o_ref[0, 0] = gt[:, 0:1] * oc_ref[0, 0] + gt[:, 1:2] * o_s + gt[:, 2:3] * o_w


def nsa_sel_sample(q8, gates8, o_c, rows_new, win_new, state_win, cache, sel_pages, sel_half, sel_ok, sel_own, *,
                   layer, batch):
    G = NSA_KV_HEADS
    wrows = state_win.shape[2]
    idx = lambda b, g, t: (b * G + g) * SEL_TOPK + t
    new = lambda c: pl.BlockSpec((1, 1, LANES), lambda b, g, pgs, hbs, oks, own: (b, 0, c + g))
    per_bg = pl.BlockSpec((1, 1, NSA_REP, LANES), lambda b, g, pgs, hbs, oks, own: (b, g, 0, 0))
    blk = lambda t: pl.BlockSpec(
        (1, 1, SEL_BLOCK * SLOTS, LANES),
        lambda b, g, pgs, hbs, oks, own: (layer, pgs[idx(b, g, t)], hbs[idx(b, g, t)], 0))
    gs = pltpu.PrefetchScalarGridSpec(
        num_scalar_prefetch=4, grid=(batch, G),
        in_specs=[per_bg, per_bg, per_bg, new(4), new(6), new(0), new(2),
                  pl.BlockSpec((1, 1, wrows, LANES), lambda b, g, pgs, hbs, oks, own: (layer, b, 0, 0))]
        + [blk(t) for t in range(SEL_TOPK)],
        out_specs=per_bg)
    return pl.pallas_call(
        _nsa_sel_sample_kernel, grid_spec=gs,
        out_shape=jax.ShapeDtypeStruct((batch, G, NSA_REP, LANES), F32),
        compiler_params=_cparams("parallel", "parallel"), name="nsa_sel_sample",
    )(sel_pages, sel_half, sel_ok, sel_own, q8, gates8, o_c, rows_new, rows_new, win_new, win_new,
      state_win, *([cache] * SEL_TOPK))


def _prep_weights(w_in, ffn1_w_down, ffn2_w_down, w_branch, w_out, nsa_cmp_w1, nsa_cmp_w2, nsa_cmp_pos):
    depth = w_in.shape[0]
    bf = lambda w: w.astype(BF16)
    src = np.zeros((NSA_KV_HEADS, 3, NSA_REP), np.int32)
    for g in range(NSA_KV_HEADS):
        for t in range(3):
            for r in range(NSA_REP):
                src[g, t, r] = (g * NSA_REP + r) * 3 + t
    w_ng = w_in[:, :, QKV_COLS:QKV_COLS + NG_COLS][:, :, src.reshape(NSA_KV_HEADS, 3 * NSA_REP)]
    w_ng = jnp.pad(w_ng, ((0, 0), (0, 0), (0, 0), (0, LANES - 3 * NSA_REP))).reshape(depth, D_MODEL, NG_PAD)
    W = CMP_STRIDE * HEAD_DIM
    return dict(
        down1=bf(ffn1_w_down), down2=bf(ffn2_w_down), ng=bf(w_ng), mg=bf(w_in[:, :, QKV_COLS + NG_COLS:]),
        branch=bf(w_branch), out=bf(w_out),
        cw0=bf(nsa_cmp_w1[:, :, :CMP_STRIDE].reshape(depth, 2, W, CMP_HIDDEN)),
        cw1=bf(nsa_cmp_w1[:, :, CMP_STRIDE:].reshape(depth, 2, W, CMP_HIDDEN)),
        cw2=bf(nsa_cmp_w2),
        cpe=nsa_cmp_pos.reshape(depth, 2, 2, W),
    )


def _ffn(x, norm_g, w_gu, w_down, layer):
    h = rmsnorm_bf16(x, norm_g)
    act = matmul_swiglu(h, w_gu, layer)
    return matmul_residual(act, w_down, layer, x, 0.5, tm=512, tn=256, name="ffn_down")


def _project(x, mix_g, w_in, wp, layer, tables, gd, gm, gn, *, seq, batch):
    h = rmsnorm_bf16(x, mix_g)
    proj = matmul(h, w_in, layer, n_cols=QKV_COLS, tn=512, name="in_proj_qkv")
    ng = matmul(h, wp['ng'], layer, n_cols=NG_PAD, tn=NG_PAD, sigmoid=True, name="in_proj_gate")
    mg = matmul(h, wp['mg'], layer, n_cols=MG_COLS, tn=512, sigmoid=True, name="in_proj_merge_gate")
    post = post_project(proj, tables, gd, gm, gn, seq=seq, batch=batch)
    return post, ng, mg


def _merge(x, o_d, o_m, o_n, mg, wp, layer):
    u = matmul_merge(o_d, o_m, o_n, wp['branch'], layer, mg)
    return matmul_residual(u, wp['out'], layer, x, 1.0, tm=1024, tn=512, name="out_proj")


def kernel(x_prompt, x_sample, cache_diff, cache_moba, cache_nsa, state_nsa_win, page_table, ffn1_norm, ffn1_w_gu, ffn1_w_down, mix_norm, w_in, diff_qk_norm, diff_lambda, diff_out_norm, moba_qk_norm, nsa_qk_norm, nsa_cmp_w1, nsa_cmp_w2, nsa_cmp_pos, w_branch, w_out, ffn2_norm, ffn2_w_gu, ffn2_w_down):
    B, T, D = x_prompt.shape
    Bs = x_sample.shape[0]
    depth = ffn1_norm.shape[0]
    n_pool = cache_diff.shape[1]
    n_pages = page_table.shape[1]
    past_len = n_pages * PAGE_SIZE
    MS = 16
    assert x_sample.shape[1] == 1 and Bs <= MS and T % MOBA_BLOCK == 0

    xp = x_prompt.reshape(B * T, D)
    xs = jnp.pad(x_sample.reshape(Bs, D), ((0, MS - Bs), (0, 0)))
    c_diff = cache_diff.reshape(depth, n_pool, PAGE_SIZE * SLOTS, HEAD_DIM)
    c_moba = cache_moba.reshape(depth, n_pool, PAGE_SIZE * SLOTS, HEAD_DIM)
    c_nsa = cache_nsa.reshape(depth, n_pool, PAGE_SIZE * SLOTS, HEAD_DIM)
    wlen = state_nsa_win.shape[2]
    s_win = state_nsa_win.reshape(depth, Bs, wlen * 2 * NSA_KV_HEADS, HEAD_DIM)
    pt_flat = page_table.reshape(-1).astype(I32)
    tab_p = _rope_tables(jnp.arange(T, dtype=I32))
    tab_s = _rope_tables(jnp.full((MS,), past_len, I32))
    wp = _prep_weights(w_in, ffn1_w_down, ffn2_w_down, w_branch, w_out, nsa_cmp_w1, nsa_cmp_w2, nsa_cmp_pos)

    outs = [[] for _ in range(8)]
    for l in range(depth):
        lam_init = 0.8 - 0.6 * math.exp(-0.3 * l)
        gd = jnp.tile(diff_qk_norm[l], (1, 2))
        gm, gn = moba_qk_norm[l], nsa_qk_norm[l]
        lam_p = diff_lambda[l]
        cmp_w = (wp['cpe'][l], wp['cw0'][l], wp['cw1'][l], wp['cw2'][l])

        xp = _ffn(xp, ffn1_norm[l], ffn1_w_gu, wp['down1'], l)
        xs = _ffn(xs, ffn1_norm[l], ffn1_w_gu, wp['down1'], l)

        post, ng, mg = _project(xp, mix_norm[l], w_in, wp, l, tab_p, gd, gm, gn, seq=T, batch=B)
        dq, rd, rdb, mq, rm, rmb, nq, rn, rnb, win, winb, xc = post
        o_d = diff_prompt(dq, rdb, lam_p, diff_out_norm[l], lam_init, batch=B, seq=T)
        o_m = moba_prompt(mq, rm, rmb, batch=B, seq=T)
        cmp_tok = compress(xc, *cmp_w)
        o_n = nsa_prompt(nq, ng, cmp_tok, rnb, winb, batch=B, seq=T)
        xp = _merge(xp, o_d, o_m, o_n, mg, wp, l)
        outs[0].append(rd.reshape(B, T, 2, DIFF_KV_HEADS, HEAD_DIM))
        outs[2].append(rm.reshape(B, T, 2, MOBA_KV_HEADS, HEAD_DIM))
        outs[4].append(rn.reshape(B, T, 4, NSA_KV_HEADS, HEAD_DIM))
        wkeep = min(WINDOW, T)
        outs[6].append(win.reshape(B, T, 2, NSA_KV_HEADS, HEAD_DIM)[:, T - wkeep:])

        post, ng, mg = _project(xs, mix_norm[l], w_in, wp, l, tab_s, gd, gm, gn, seq=1, batch=Bs)
        dq, rd, rdb, mq, rm, rmb, nq, rn, rnb, win, winb = post
        q5 = dq[:Bs].reshape(Bs, DIFF_KV_HEADS, 2, 2, DIFF_DH)
        eye_c = jnp.eye(2, dtype=BF16)
        q16 = (q5[:, :, :, :, None, :] * eye_c[None, None, None, :, :, None]).reshape(Bs, 16, LANES)
        kv4 = rdb[:Bs].reshape(Bs, 2, DIFF_KV_HEADS, 1, LANES)
        kvn16 = jnp.broadcast_to(kv4, (Bs, 2, DIFF_KV_HEADS, 4, LANES)).reshape(Bs, 2, 16, LANES)
        o_d = diff_sample(q16, kvn16[:, 0], kvn16[:, 1], c_diff, pt_flat, lam_p, diff_out_norm[l], lam_init,
                          layer=l, batch=Bs, n_pages=n_pages)
        msel = moba_gate_sample(mq[:Bs].reshape(Bs, MOBA_HEADS, LANES), c_moba, pt_flat,
                                layer=l, batch=Bs, n_pages=n_pages)
        m_idx = msel[:, :, 0:MOBA_TOPK]
        m_ok = msel[:, :, 4:4 + MOBA_TOPK]
        ppb = MOBA_BLOCK // PAGE_SIZE
        logical = (m_idx[..., None] * ppb + jnp.arange(ppb, dtype=I32)).reshape(Bs, MOBA_HEADS * MOBA_TOPK * ppb)
        m_pages = jnp.take_along_axis(page_table.astype(I32), logical, axis=1)
        q8 = jnp.broadcast_to(mq[:Bs].reshape(Bs, MOBA_HEADS, 1, HEAD_DIM), (Bs, MOBA_HEADS, 8, HEAD_DIM))
        o_m = moba_sample(q8, rmb[:Bs].reshape(Bs, 1, 1024), c_moba, m_pages.reshape(-1), m_ok.reshape(-1),
                          layer=l, batch=Bs)
        xcs = nsa_chunk_sample(c_nsa, pt_flat, layer=l, batch=Bs, n_pages=n_pages)
        cmp_tok = compress(xcs, *cmp_w)
        nq8 = nq[:Bs].reshape(Bs, NSA_KV_HEADS, NSA_REP, HEAD_DIM)
        o_c, nsel = nsa_cmp_sample(nq8, cmp_tok, batch=Bs, past_len=past_len)
        n_idx = nsel[:, :, 0, 0:SEL_TOPK]
        n_ok = nsel[:, :, 0, SEL_TOPK:2 * SEL_TOPK]
        bpp = PAGE_SIZE // SEL_BLOCK
        in_past = n_idx < past_len // SEL_BLOCK
        n_log = jnp.where(in_past, n_idx // bpp, 0)
        n_pages_sel = jnp.take_along_axis(page_table.astype(I32), n_log.reshape(Bs, -1), axis=1)
        n_half = jnp.where(in_past, n_idx % bpp, 0)
        n_on = (n_ok > 0) & in_past
        n_own = jnp.any((n_ok > 0) & (n_idx == past_len // SEL_BLOCK), axis=-1)
        gates8 = jnp.pad(ng[:Bs].reshape(Bs, NSA_KV_HEADS, LANES)[:, :, :3 * NSA_REP]
                         .reshape(Bs, NSA_KV_HEADS, 3, NSA_REP).transpose(0, 1, 3, 2),
                         ((0, 0), (0, 0), (0, 0), (0, LANES - 3)))
        o_n = nsa_sel_sample(nq8, gates8, o_c, rnb[:Bs].reshape(Bs, 1, 1024), winb[:Bs].reshape(Bs, 1, 512),
                             s_win, c_nsa, n_pages_sel.reshape(-1), n_half.reshape(-1).astype(I32),
                             n_on.reshape(-1).astype(I32), n_own.reshape(-1).astype(I32), layer=l, batch=Bs)
        pad_rows = lambda a, w: jnp.pad(a.reshape(Bs, w).astype(BF16), ((0, MS - Bs), (0, 0)))
        xs = _merge(xs, pad_rows(o_d, 1024), pad_rows(o_m, 1024), pad_rows(o_n, 2048), mg, wp, l)
        outs[1].append(rd[:Bs].reshape(Bs, 1, 2, DIFF_KV_HEADS, HEAD_DIM))
        outs[3].append(rm[:Bs].reshape(Bs, 1, 2, MOBA_KV_HEADS, HEAD_DIM))
        outs[5].append(rn[:Bs].reshape(Bs, 1, 4, NSA_KV_HEADS, HEAD_DIM))
        new_win = win[:Bs].reshape(Bs, 1, 2, NSA_KV_HEADS, HEAD_DIM)
        outs[7].append(jnp.concatenate([state_nsa_win[l], new_win], axis=1)[:, 1:])

        xp = _ffn(xp, ffn2_norm[l], ffn2_w_gu, wp['down2'], l)
        xs = _ffn(xs, ffn2_norm[l], ffn2_w_gu, wp['down2'], l)

    return (xp.reshape(B, T, D), xs[:Bs].reshape(Bs, 1, D)) + tuple(jnp.stack(o) for o in outs)
```

```python
import functools
import math

import jax
import jax.numpy as jnp
import numpy as np
from jax import lax
from jax.experimental import pallas as pl
from jax.experimental.pallas import tpu as pltpu

F32 = jnp.float32
BF16 = jnp.bfloat16
I32 = jnp.int32

D_MODEL = 4096
D_FF = 11008
HEAD_DIM = 128
PAGE_SIZE = 128
ROPE_THETA = 10000.0
EPS = 1e-6
NEG = -1e30
TINY = 1e-30
DIFF_HEADS, DIFF_KV_HEADS, DIFF_DH = 8, 4, 64
MOBA_HEADS, MOBA_KV_HEADS, MOBA_BLOCK, MOBA_TOPK = 8, 4, 256, 3
NSA_HEADS, NSA_KV_HEADS = 16, 2
NSA_REP = NSA_HEADS // NSA_KV_HEADS
CMP_BLOCK, CMP_STRIDE, CMP_HIDDEN = 32, 16, 256
SEL_BLOCK, SEL_TOPK, WINDOW = 64, 16, 512
FORCE_SCORE = 1e4
QKV_COLS = 7680
NG_COLS = NSA_HEADS * 3
NG_PAD = NSA_KV_HEADS * 128
MG_COLS = 3 * D_MODEL

LANES = 128
SUBLANES = 8
VMEM_LIMIT_BYTES = 56 * 1024 * 1024
SCALE128 = HEAD_DIM ** -0.5
SLOTS = 8


def _cparams(*sem):
    return pltpu.CompilerParams(dimension_semantics=sem, vmem_limit_bytes=VMEM_LIMIT_BYTES)


def _dot(a, b):
    return jnp.dot(a, b, preferred_element_type=F32)


def _dot_nt(a, b):
    return lax.dot_general(a, b, (((1,), (1,)), ((), ())), preferred_element_type=F32)


def _dot_exact(a, b):
    return jnp.dot(a, b, preferred_element_type=F32, precision=lax.Precision.HIGHEST)


def _rmsnorm_kernel(x_ref, g_ref, o_ref):
    x = x_ref[...]
    ms = jnp.mean(x * x, axis=-1, keepdims=True)
    o_ref[...] = (x * lax.rsqrt(ms + EPS) * g_ref[...]).astype(o_ref.dtype)


def rmsnorm_bf16(x, g):
    M, D = x.shape
    tm = min(M, 256)
    return pl.pallas_call(
        _rmsnorm_kernel, grid=(M // tm,),
        in_specs=[pl.BlockSpec((tm, D), lambda i: (i, 0)), pl.BlockSpec((1, D), lambda i: (0, 0))],
        out_specs=pl.BlockSpec((tm, D), lambda i: (i, 0)),
        out_shape=jax.ShapeDtypeStruct((M, D), BF16),
        compiler_params=_cparams("parallel"), name="rmsnorm",
    )(x, g.reshape(1, D))


def _mm_kernel(a_ref, b_ref, o_ref, *, sigmoid):
    acc = _dot(a_ref[...], b_ref[...].astype(BF16))
    if sigmoid:
        acc = jax.nn.sigmoid(acc)
    o_ref[...] = acc.astype(o_ref.dtype)


def matmul(a, w, layer, *, n_cols, tn, out_dtype=F32, sigmoid=False, name="matmul"):
    M, K = a.shape
    tm = min(M, 1024)
    return pl.pallas_call(
        functools.partial(_mm_kernel, sigmoid=sigmoid), grid=(M // tm, n_cols // tn),
        in_specs=[pl.BlockSpec((tm, K), lambda i, j: (i, 0)),
                  pl.BlockSpec((None, K, tn), lambda i, j: (layer, 0, j))],
        out_specs=pl.BlockSpec((tm, tn), lambda i, j: (i, j)),
        out_shape=jax.ShapeDtypeStruct((M, n_cols), out_dtype),
        compiler_params=_cparams("parallel", "parallel"), name=name,
    )(a, w)


def _mm_swiglu_kernel(a_ref, bg_ref, bu_ref, o_ref):
    a = a_ref[...]
    g = _dot(a, bg_ref[...].astype(BF16))
    u = _dot(a, bu_ref[...].astype(BF16))
    o_ref[...] = (g * jax.nn.sigmoid(g) * u).astype(o_ref.dtype)


def matmul_swiglu(a, w_gu, layer, *, tn=256):
    M, K = a.shape
    F = w_gu.shape[2] // 2
    tm = min(M, 1024)
    nf = F // tn
    return pl.pallas_call(
        _mm_swiglu_kernel, grid=(M // tm, nf),
        in_specs=[pl.BlockSpec((tm, K), lambda i, j: (i, 0)),
                  pl.BlockSpec((None, K, tn), lambda i, j: (layer, 0, j)),
                  pl.BlockSpec((None, K, tn), lambda i, j: (layer, 0, j + nf))],
        out_specs=pl.BlockSpec((tm, tn), lambda i, j: (i, j)),
        out_shape=jax.ShapeDtypeStruct((M, F), BF16),
        compiler_params=_cparams("parallel", "parallel"), name="ffn_gate_up",
    )(a, w_gu, w_gu)


def _mm_res_kernel(a_ref, b_ref, x_ref, o_ref, *, alpha):
    o_ref[...] = x_ref[...] + alpha * _dot(a_ref[...], b_ref[...].astype(BF16))


def matmul_residual(a, w, layer, x, alpha, *, tm, tn, name):
    M, K = a.shape
    N = w.shape[2]
    tm = min(M, tm)
    return pl.pallas_call(
        functools.partial(_mm_res_kernel, alpha=alpha), grid=(M // tm, N // tn),
        in_specs=[pl.BlockSpec((tm, K), lambda i, j: (i, 0)),
                  pl.BlockSpec((None, K, tn), lambda i, j: (layer, 0, j)),
                  pl.BlockSpec((tm, tn), lambda i, j: (i, j))],
        out_specs=pl.BlockSpec((tm, tn), lambda i, j: (i, j)),
        out_shape=jax.ShapeDtypeStruct((M, N), F32),
        compiler_params=_cparams("parallel", "parallel"), name=name,
    )(a, w, x)


def _mm_merge_kernel(od_ref, om_ref, on_ref, wd_ref, wm_ref, wn_ref, g0_ref, g1_ref, g2_ref, o_ref):
    u = g0_ref[...] * _dot(od_ref[...], wd_ref[...])
    u = u + g1_ref[...] * _dot(om_ref[...], wm_ref[...])
    u = u + g2_ref[...] * _dot(on_ref[...], wn_ref[...])
    o_ref[...] = u.astype(o_ref.dtype)


def matmul_merge(o_d, o_m, o_n, w_branch, layer, mg, *, tn=512):
    M = o_d.shape[0]
    D = w_branch.shape[2]
    tm = min(M, 1024)
    wd, wm, wn = o_d.shape[1], o_m.shape[1], o_n.shape[1]
    nd = D // tn
    return pl.pallas_call(
        _mm_merge_kernel, grid=(M // tm, nd),
        in_specs=[pl.BlockSpec((tm, wd), lambda i, j: (i, 0)),
                  pl.BlockSpec((tm, wm), lambda i, j: (i, 0)),
                  pl.BlockSpec((tm, wn), lambda i, j: (i, 0)),
                  pl.BlockSpec((None, wd, tn), lambda i, j: (layer, 0, j)),
                  pl.BlockSpec((None, wm, tn), lambda i, j: (layer, 1, j)),
                  pl.BlockSpec((None, wn, tn), lambda i, j: (layer, 1, j)),
                  pl.BlockSpec((tm, tn), lambda i, j: (i, j)),
                  pl.BlockSpec((tm, tn), lambda i, j: (i, j + nd)),
                  pl.BlockSpec((tm, tn), lambda i, j: (i, j + 2 * nd))],
        out_specs=pl.BlockSpec((tm, tn), lambda i, j: (i, j)),
        out_shape=jax.ShapeDtypeStruct((M, D), BF16),
        compiler_params=_cparams("parallel", "parallel"), name="merge",
    )(o_d, o_m, o_n, w_branch, w_branch, w_branch, mg, mg, mg)


def _post_kernel(p_ref, c128_ref, s128_ref, c64_ref, s64_ref, gd_ref, gm_ref, gn_ref,
                 dq_ref, rd_ref, rdb_ref, mq_ref, rm_ref, rmb_ref, nq_ref, rn_ref, rnb_ref,
                 win_ref, winb_ref, *rest, tm, emit_chunks):
    lane = lax.broadcasted_iota(I32, (tm, LANES), 1)
    lo = lane < 64
    first_half64 = (lane & 63) < 32
    c128, s128 = c128_ref[...], s128_ref[...]
    c64, s64 = c64_ref[...], s64_ref[...]

    def slab(k):
        return p_ref[:, k * LANES:(k + 1) * LANES]

    def nr128(x, g):
        ms = jnp.mean(x * x, axis=-1, keepdims=True)
        y = x * lax.rsqrt(ms + EPS) * g
        return y * c128 + pltpu.roll(y, 64, 1) * s128

    def nr64(x, g):
        x2 = x * x
        s_lo = jnp.sum(jnp.where(lo, x2, 0.0), axis=-1, keepdims=True)
        s_hi = jnp.sum(jnp.where(lo, 0.0, x2), axis=-1, keepdims=True)
        ms = jnp.where(lo, s_lo, s_hi) * (1.0 / 64.0)
        y = x * lax.rsqrt(ms + EPS) * g
        partner = jnp.where(first_half64, pltpu.roll(y, 96, 1), pltpu.roll(y, 32, 1))
        return y * c64 + partner * s64

    def put(ref_f, ref_b, k, val):
        if ref_f is not None:
            ref_f[:, k * LANES:(k + 1) * LANES] = val
        if ref_b is not None:
            ref_b[:, k * LANES:(k + 1) * LANES] = val.astype(BF16)

    gdq, gdk = gd_ref[0:1, :], gd_ref[1:2, :]
    gmq, gmk = gm_ref[0:1, :], gm_ref[1:2, :]
    for h in range(8):
        put(None, dq_ref, h, nr64(slab(h), gdq) * 0.125)
    for h in range(4):
        put(rd_ref, rdb_ref, h, nr64(slab(8 + h), gdk))
        put(rd_ref, rdb_ref, 4 + h, slab(12 + h))
    for h in range(8):
        put(None, mq_ref, h, nr128(slab(16 + h), gmq) * SCALE128)
    for h in range(4):
        put(rm_ref, rmb_ref, h, nr128(slab(24 + h), gmk))
        put(rm_ref, rmb_ref, 4 + h, slab(28 + h))
    for h in range(16):
        put(None, nq_ref, h, nr128(slab(32 + h), gn_ref[0:1, :]) * SCALE128)
    for g in range(2):
        put(rn_ref, rnb_ref, g, nr128(slab(48 + g), gn_ref[1:2, :]))
        put(rn_ref, rnb_ref, 2 + g, slab(50 + g))
        put(rn_ref, rnb_ref, 4 + g, nr128(slab(52 + g), gn_ref[2:3, :]))
        put(rn_ref, rnb_ref, 6 + g, slab(54 + g))
        put(win_ref, winb_ref, g, nr128(slab(56 + g), gn_ref[3:4, :]))
        put(win_ref, winb_ref, 2 + g, slab(58 + g))
    if emit_chunks:
        xc_ref, slab_ref = rest
        n = tm // CMP_STRIDE
        for cg in range(4):
            slab_ref[...] = rn_ref[:, cg * LANES:(cg + 1) * LANES]
            for j in range(CMP_STRIDE):
                xc_ref[0, cg, :, j * LANES:(j + 1) * LANES] = slab_ref[pl.ds(j, n, stride=CMP_STRIDE), :]


def _rope_tables(pos):
    def tab(d):
        inv = ROPE_THETA ** (-jnp.arange(0, d, 2, dtype=F32) / d)
        ang = pos.astype(F32)[:, None] * inv[None, :]
        cos, sin = jnp.cos(ang), jnp.sin(ang)
        c = jnp.concatenate([cos, cos], axis=-1)
        s = jnp.concatenate([-sin, sin], axis=-1)
        rep = LANES // d
        return jnp.tile(c, (1, rep)), jnp.tile(s, (1, rep))
    c128, s128 = tab(128)
    c64, s64 = tab(64)
    return c128, s128, c64, s64


def post_project(proj, tables, gd, gm, gn, *, seq, batch):
    M = proj.shape[0]
    tm = min(M, 256)
    emit_chunks = seq >= tm
    nt = max(seq // tm, 1)
    row = lambda w: pl.BlockSpec((tm, w), lambda i: (i, 0))
    tab = pl.BlockSpec((tm, LANES), lambda i: (i % nt, 0))
    par = lambda r: pl.BlockSpec((r, LANES), lambda i: (0, 0))
    out_shapes = [
        jax.ShapeDtypeStruct((M, 1024), BF16),
        jax.ShapeDtypeStruct((M, 1024), F32), jax.ShapeDtypeStruct((M, 1024), BF16),
        jax.ShapeDtypeStruct((M, 1024), BF16),
        jax.ShapeDtypeStruct((M, 1024), F32), jax.ShapeDtypeStruct((M, 1024), BF16),
        jax.ShapeDtypeStruct((M, 2048), BF16),
        jax.ShapeDtypeStruct((M, 1024), F32), jax.ShapeDtypeStruct((M, 1024), BF16),
        jax.ShapeDtypeStruct((M, 512), F32), jax.ShapeDtypeStruct((M, 512), BF16),
    ]
    out_specs = [row(1024), row(1024), row(1024), row(1024), row(1024), row(1024), row(2048),
                 row(1024), row(1024), row(512), row(512)]
    if emit_chunks:
        nchunk = seq // CMP_STRIDE
        out_shapes.append(jax.ShapeDtypeStruct((batch, 4, nchunk, CMP_STRIDE * LANES), F32))
        out_specs.append(pl.BlockSpec((1, 4, tm // CMP_STRIDE, CMP_STRIDE * LANES),
                                      lambda i: (i // nt, 0, i % nt, 0)))
    return pl.pallas_call(
        functools.partial(_post_kernel, tm=tm, emit_chunks=emit_chunks), grid=(M // tm,),
        in_specs=[row(QKV_COLS), tab, tab, tab, tab, par(2), par(2), par(4)],
        out_specs=out_specs, out_shape=out_shapes,
        scratch_shapes=[pltpu.VMEM((tm, LANES), F32)] if emit_chunks else [],
        compiler_params=_cparams("parallel"), name="post_project",
    )(proj, *tables, gd, gm, gn)


def _online(carry, s, v, ok=None):
    m, l, acc = carry
    m_new = jnp.maximum(m, jnp.max(s, axis=-1, keepdims=True))
    alpha = jnp.exp(m - m_new)
    p = jnp.exp(s - m_new)
    if ok is not None:
        p = jnp.where(ok, p, 0.0)
    l = alpha * l + jnp.sum(p, axis=-1, keepdims=True)
    acc = alpha * acc + _dot(p.astype(BF16), v)
    return m_new, l, acc


def _init_carry(rows, width):
    return (jnp.full((rows, 1), NEG, F32), jnp.zeros((rows, 1), F32), jnp.zeros((rows, width), F32))


def _online_t(carry, s_t, v, ok=None):
    m, l, acc = carry
    m_new = jnp.maximum(m, jnp.max(s_t, axis=0, keepdims=True))
    alpha = jnp.exp(m - m_new)
    p = jnp.exp(s_t - m_new)
    if ok is not None:
        p = jnp.where(ok, p, 0.0)
    l = alpha * l + jnp.sum(p, axis=0, keepdims=True)
    acc = alpha * acc + _dot(v.astype(F32).T.astype(BF16), p.astype(BF16))
    return m_new, l, acc


def _init_carry_t(rows, width):
    return (jnp.full((1, rows), NEG, F32), jnp.zeros((1, rows), F32), jnp.zeros((width, rows), F32))


def _flash_reset(m_ref, l_ref, acc_ref):
    m_ref[...] = jnp.full(m_ref.shape, NEG, F32)
    l_ref[...] = jnp.zeros(l_ref.shape, F32)
    acc_ref[...] = jnp.zeros(acc_ref.shape, F32)


def _lambda(lam_ref, lam_init):
    lq = lam_ref[...]
    a = jnp.sum(lq[0:1, :] * lq[1:2, :], axis=-1, keepdims=True)
    b = jnp.sum(lq[2:3, :] * lq[3:4, :], axis=-1, keepdims=True)
    return jnp.exp(a) - jnp.exp(b) + lam_init


def _head_norm(o, g, post):
    ms = jnp.mean(o * o, axis=-1, keepdims=True)
    return o * lax.rsqrt(ms + EPS) * g * post


def _diff_prompt_kernel(lam_ref, gn_ref, q_ref, k_ref, v_ref, o_ref, *, tq, lam_init):
    i = pl.program_id(2)
    lane = lax.broadcasted_iota(I32, (tq, LANES), 1)
    q = q_ref[...]
    zero = jnp.zeros((tq, LANES), BF16)
    parts = []
    for r in range(2):
        qr = q[:, r * LANES:(r + 1) * LANES]
        parts.append(jnp.where(lane < 64, qr, zero))
        parts.append(jnp.where(lane < 64, zero, qr))
    Q = jnp.concatenate(parts, axis=0)
    R = 4 * tq

    def blk(kb):
        s0 = pl.multiple_of(kb * tq, tq)
        return k_ref[pl.ds(s0, tq), :], v_ref[pl.ds(s0, tq), :]

    def body(kb, carry):
        k, v = blk(kb)
        return _online_t(carry, _dot_nt(k, Q), v)

    carry = lax.fori_loop(0, i, body, _init_carry_t(R, LANES))
    k, v = blk(i)
    key = lax.broadcasted_iota(I32, (tq, R), 0)
    qrow = lax.broadcasted_iota(I32, (tq, R), 1) & (tq - 1)
    s = jnp.where(key <= qrow, _dot_nt(k, Q), NEG)
    m, l, acc = _online_t(carry, s, v)
    o = acc / jnp.maximum(l, TINY)
    lam = _lambda(lam_ref, lam_init)
    g = gn_ref[...]
    for r in range(2):
        o_r = o[:, (2 * r) * tq:(2 * r + 1) * tq] - lam * o[:, (2 * r + 1) * tq:(2 * r + 2) * tq]
        o_ref[:, r * LANES:(r + 1) * LANES] = _head_norm(o_r.T, g, 1.0 - lam_init).astype(o_ref.dtype)


def diff_prompt(dq, rows_b, lam_param, out_norm, lam_init, *, batch, seq):
    M = dq.shape[0]
    tq = 256
    nq = seq // tq
    return pl.pallas_call(
        functools.partial(_diff_prompt_kernel, tq=tq, lam_init=lam_init),
        grid=(batch, DIFF_KV_HEADS, nq),
        in_specs=[pl.BlockSpec((4, DIFF_DH), lambda b, g, i: (0, 0)),
                  pl.BlockSpec((1, LANES), lambda b, g, i: (0, 0)),
                  pl.BlockSpec((tq, 2 * LANES), lambda b, g, i: (b * nq + i, g)),
                  pl.BlockSpec((seq, LANES), lambda b, g, i: (b, g)),
                  pl.BlockSpec((seq, LANES), lambda b, g, i: (b, DIFF_KV_HEADS + g))],
        out_specs=pl.BlockSpec((tq, 2 * LANES), lambda b, g, i: (b * nq + i, g)),
        out_shape=jax.ShapeDtypeStruct((M, DIFF_HEADS * LANES), BF16),
        compiler_params=_cparams("parallel", "parallel", "parallel"), name="diff_prompt",
    )(lam_param, out_norm.reshape(1, LANES), dq, rows_b, rows_b)


def _rank_lt(vals, n_cand, lane):
    cnt = jnp.zeros(vals.shape, F32)
    for m in range(n_cand):
        col = vals[:, m:m + 1]
        beats = (col > vals) | ((col == vals) & (lane > m))
        cnt = cnt + jnp.where(beats, 1.0, 0.0)
    return cnt


def _rank_lt_t(vals, n_cand, idx):
    cnt = jnp.zeros(vals.shape, F32)
    for m in range(n_cand):
        row = vals[m:m + 1, :]
        beats = (row > vals) | ((row == vals) & (idx > m))
        cnt = cnt + jnp.where(beats, 1.0, 0.0)
    return cnt


def _moba_prompt_kernel(q_ref, kf_ref, k_ref, v_ref, o_ref, *, tq, nb):
    i = pl.program_id(2)
    q = q_ref[...]
    Q = jnp.concatenate([q[:, :LANES], q[:, LANES:]], axis=0)
    R = 2 * tq
    nbp = -(-nb // SUBLANES) * SUBLANES
    blkid = lax.broadcasted_iota(I32, (nbp, R), 0)
    kms = [jnp.mean(kf_ref[n * MOBA_BLOCK:(n + 1) * MOBA_BLOCK, :], axis=0, keepdims=True) for n in range(nb)]
    kmean = jnp.concatenate(kms + [jnp.zeros((LANES - nb, LANES), F32)], axis=0)
    gate = _dot_nt(kmean.astype(BF16), Q)[0:nbp]
    gate = jnp.where(blkid < i, gate, NEG)
    cnt = _rank_lt_t(gate, nb, blkid)
    sel = jnp.where((cnt < MOBA_TOPK) & (gate > NEG / 2), 1.0, 0.0)

    def blk(kb):
        s0 = pl.multiple_of(kb * tq, tq)
        return k_ref[pl.ds(s0, tq), :], v_ref[pl.ds(s0, tq), :]

    def body(kb, carry):
        k, v = blk(kb)
        on = jnp.sum(jnp.where(blkid == kb, sel, 0.0), axis=0, keepdims=True) > 0.5
        s = jnp.where(on, _dot_nt(k, Q), NEG)
        return _online_t(carry, s, v, ok=on)

    carry = lax.fori_loop(0, i, body, _init_carry_t(R, LANES))
    k, v = blk(i)
    key = lax.broadcasted_iota(I32, (tq, R), 0)
    qrow = lax.broadcasted_iota(I32, (tq, R), 1) & (tq - 1)
    s = jnp.where(key <= qrow, _dot_nt(k, Q), NEG)
    m, l, acc = _online_t(carry, s, v)
    o = acc / jnp.maximum(l, TINY)
    for r in range(2):
        o_ref[:, r * LANES:(r + 1) * LANES] = o[:, r * tq:(r + 1) * tq].T.astype(o_ref.dtype)


def moba_prompt(mq, rows_f, rows_b, *, batch, seq):
    M = mq.shape[0]
    tq = MOBA_BLOCK
    nq = seq // tq
    return pl.pallas_call(
        functools.partial(_moba_prompt_kernel, tq=tq, nb=seq // MOBA_BLOCK),
        grid=(batch, MOBA_KV_HEADS, nq),
        in_specs=[pl.BlockSpec((tq, 2 * LANES), lambda b, g, i: (b * nq + i, g)),
                  pl.BlockSpec((seq, LANES), lambda b, g, i: (b, g)),
                  pl.BlockSpec((seq, LANES), lambda b, g, i: (b, g)),
                  pl.BlockSpec((seq, LANES), lambda b, g, i: (b, MOBA_KV_HEADS + g))],
        out_specs=pl.BlockSpec((tq, 2 * LANES), lambda b, g, i: (b * nq + i, g)),
        out_shape=jax.ShapeDtypeStruct((M, MOBA_HEADS * LANES), BF16),
        compiler_params=_cparams("parallel", "parallel", "parallel"), name="moba_prompt",
    )(mq, rows_f, rows_b, rows_b)


def _compress_kernel(x_ref, pe_ref, w0_ref, w1_ref, w2_ref, o_ref, *, n):
    x = x_ref[0, 0]
    y0 = _dot((x + pe_ref[0, 0:1, :]).astype(BF16), w0_ref[0])
    y1 = _dot((x + pe_ref[0, 1:2, :]).astype(BF16), w1_ref[0])
    hid = y0 + pltpu.roll(y1, n - 1, 0)
    act = hid * jax.nn.sigmoid(hid)
    o_ref[0, 0] = _dot(act.astype(BF16), w2_ref[0]).astype(o_ref.dtype)


def compress(xc, pe, w0, w1, w2):
    B, _, n, W = xc.shape
    return pl.pallas_call(
        functools.partial(_compress_kernel, n=n), grid=(B, 4),
        in_specs=[pl.BlockSpec((1, 1, n, W), lambda b, c: (b, c, 0, 0)),
                  pl.BlockSpec((1, 2, W), lambda b, c: (c // 2, 0, 0)),
                  pl.BlockSpec((1, W, CMP_HIDDEN), lambda b, c: (c // 2, 0, 0)),
                  pl.BlockSpec((1, W, CMP_HIDDEN), lambda b, c: (c // 2, 0, 0)),
                  pl.BlockSpec((1, CMP_HIDDEN, LANES), lambda b, c: (c // 2, 0, 0))],
        out_specs=pl.BlockSpec((1, 1, n, LANES), lambda b, c: (b, c, 0, 0)),
        out_shape=jax.ShapeDtypeStruct((B, 4, n, LANES), BF16),
        compiler_params=_cparams("parallel", "parallel"), name="compress",
    )(xc, pe, w0, w1, w2)


def _nsa_prompt_kernel(q_ref, g_ref, ck_ref, cv_ref, sk_ref, sv_ref, wk_ref, wv_ref, imp_ref, exp_ref,
                       o_ref, msel_ref, *, tq, seq):
    i = pl.program_id(2)
    H = NSA_REP
    R = H * tq
    n_cmp = seq // CMP_STRIDE - 1
    n_sel = seq // SEL_BLOCK
    q = q_ref[...]
    Q = jnp.concatenate([q[:, r * LANES:(r + 1) * LANES] for r in range(H)], axis=0)
    row = lax.broadcasted_iota(I32, (tq, R), 0)
    qp = i * tq + (lax.broadcasted_iota(I32, (tq, R), 1) & (tq - 1))

    s = _dot_nt(ck_ref[0, 0], Q)
    ok = (row < n_cmp) & (row * CMP_STRIDE + (CMP_BLOCK - 1) <= qp)
    s = jnp.where(ok, s, NEG)
    p = jnp.where(ok, jnp.exp(s - jnp.max(s, axis=0, keepdims=True)), 0.0)
    pc = p / jnp.maximum(jnp.sum(p, axis=0, keepdims=True), TINY)
    o_c = _dot(cv_ref[0, 0].astype(F32).T.astype(BF16), pc.astype(BF16))

    pcs = pc[:, 0:tq]
    for r in range(1, H):
        pcs = pcs + pc[:, r * tq:(r + 1) * tq]
    imp = _dot_exact(imp_ref[...], pcs)[0:n_sel]
    blk_id = lax.broadcasted_iota(I32, (n_sel, tq), 0)
    cur = (i * tq + lax.broadcasted_iota(I32, (n_sel, tq), 1)) >> 6
    valid = blk_id <= cur
    forced = (blk_id == 0) | (blk_id == cur) | (blk_id == cur - 1)
    imp = jnp.where(valid, jnp.where(forced, FORCE_SCORE, imp), NEG)
    cnt = _rank_lt_t(imp, n_sel, blk_id)
    sel = jnp.where((cnt < SEL_TOPK) & (imp > NEG / 2), 1.0, 0.0)
    sel = jnp.concatenate([sel, jnp.zeros((LANES - n_sel, tq), F32)], axis=0)
    msel_ref[...] = _dot(exp_ref[...], sel.astype(BF16))

    tk = 2 * tq
    row_k = lax.broadcasted_iota(I32, (tk, tq), 0)
    qp_k = i * tq + lax.broadcasted_iota(I32, (tk, tq), 1)

    def blk(kref, vref, kb):
        s0 = pl.multiple_of(kb * tk, tk)
        return kref[pl.ds(s0, tk), :], vref[pl.ds(s0, tk), :]

    def attend(carry, k, v, ok_q):
        bias = jnp.concatenate([jnp.where(ok_q, 0.0, NEG)] * H, axis=1)
        return _online_t(carry, _dot_nt(k, Q) + bias, v)

    def sel_body(kb, carry):
        k, v = blk(sk_ref, sv_ref, kb)
        on = msel_ref[pl.ds(pl.multiple_of(kb * tk, tk), tk), :] > 0.5
        return attend(carry, k, v, on & (kb * tk + row_k <= qp_k))

    m, l, acc = lax.fori_loop(0, (i + 2) // 2, sel_body, _init_carry_t(R, LANES))
    o_s = acc / jnp.maximum(l, TINY)

    def win_body(kb, carry):
        k, v = blk(wk_ref, wv_ref, kb)
        dist = qp_k - (kb * tk + row_k)
        return attend(carry, k, v, (dist >= 0) & (dist <= WINDOW))

    first = jnp.maximum(i - WINDOW // tq, 0) // 2
    m, l, acc = lax.fori_loop(first, i // 2 + 1, win_body, _init_carry_t(R, LANES))
    o_w = acc / jnp.maximum(l, TINY)

    gates = g_ref[...].T
    for r in range(H):
        cs = slice(r * tq, (r + 1) * tq)
        o = (gates[r:r + 1] * o_c[:, cs] + gates[H + r:H + r + 1] * o_s[:, cs]
             + gates[2 * H + r:2 * H + r + 1] * o_w[:, cs])
        o_ref[:, r * LANES:(r + 1) * LANES] = o.T.astype(o_ref.dtype)


def _importance_matrix(n_tok_pad, n_tok, n_blk_pad, n_blk):
    r_ = SEL_BLOCK // CMP_STRIDE
    f_ = CMP_BLOCK // CMP_STRIDE - 1
    i = np.arange(n_tok_pad)[:, None]
    j = np.arange(n_blk_pad)[None, :]
    a = (i >= r_ * j - f_) & (i <= r_ * j + r_ - 1) & (i < n_tok) & (j < n_blk)
    return jnp.asarray(a.astype(np.float32))


def _expand_matrix(n_blk_pad, n_keys):
    j = np.arange(n_blk_pad)[:, None]
    t = np.arange(n_keys)[None, :]
    return jnp.asarray((t // SEL_BLOCK == j).astype(np.float32), dtype=BF16)


def nsa_prompt(nq, ng, cmp_tok, rows_b, win_b, *, batch, seq):
    M = nq.shape[0]
    tq = 128
    nt = seq // tq
    G = NSA_KV_HEADS
    n_tok = seq // CMP_STRIDE
    assert n_tok == LANES and tq == LANES and (seq // SEL_BLOCK) % SUBLANES == 0
    imp_m = _importance_matrix(LANES, n_tok - 1, LANES, seq // SEL_BLOCK).T
    exp_m = _expand_matrix(LANES, seq).T
    kv = lambda c: pl.BlockSpec((seq, LANES), lambda b, g, i: (b, c + g))
    return pl.pallas_call(
        functools.partial(_nsa_prompt_kernel, tq=tq, seq=seq), grid=(batch, G, nt),
        in_specs=[pl.BlockSpec((tq, NSA_REP * LANES), lambda b, g, i: (b * nt + i, g)),
                  pl.BlockSpec((tq, LANES), lambda b, g, i: (b * nt + i, g)),
                  pl.BlockSpec((1, 1, n_tok, LANES), lambda b, g, i: (b, g, 0, 0)),
                  pl.BlockSpec((1, 1, n_tok, LANES), lambda b, g, i: (b, G + g, 0, 0)),
                  kv(4), kv(6), kv(0), kv(2),
                  pl.BlockSpec((LANES, LANES), lambda b, g, i: (0, 0)),
                  pl.BlockSpec((seq, LANES), lambda b, g, i: (0, 0))],
        out_specs=pl.BlockSpec((tq, NSA_REP * LANES), lambda b, g, i: (b * nt + i, g)),
        out_shape=jax.ShapeDtypeStruct((M, NSA_HEADS * LANES), BF16),
        scratch_shapes=[pltpu.VMEM((seq, tq), F32)],
        compiler_params=_cparams("parallel", "parallel", "arbitrary"), name="nsa_prompt",
    )(nq, ng, cmp_tok, cmp_tok, rows_b, rows_b, win_b, win_b, imp_m, exp_m)


def _slot_scores(q, raw_bf16, key_slot, nslots):
    s = _dot_nt(q, raw_bf16)
    col = lax.broadcasted_iota(I32, s.shape, 1) & (nslots - 1)
    return s, col == key_slot


def _slot_update(carry, s, ok, raw_bf16, shift):
    m_old, l, acc = carry
    m_new = jnp.maximum(m_old, jnp.max(s, axis=-1, keepdims=True))
    alpha = jnp.exp(m_old - m_new)
    p = jnp.where(ok, jnp.exp(s - m_new), 0.0)
    l = alpha * l + jnp.sum(p, axis=-1, keepdims=True)
    acc = alpha * acc + _dot(pltpu.roll(p, shift, 1).astype(BF16), raw_bf16)
    return m_new, l, acc


def _with_new(m, l, acc, s_new, vn, on=True):
    s_new = jnp.where(on, s_new, NEG)
    m2 = jnp.maximum(m, s_new)
    a2 = jnp.exp(m - m2)
    pn = jnp.where(on, jnp.exp(s_new - m2), 0.0)
    return (a2 * acc + pn * vn) / jnp.maximum(a2 * l + pn, TINY)


def _diff_sample_kernel(pt_ref, lam_ref, gn_ref, q_ref, kn_ref, vn_ref, *refs, P, lam_init):
    pages = refs[:P]
    o_ref = refs[P]
    m_ref, l_ref, acc_ref = refs[P + 1:]
    j = pl.program_id(1)

    @pl.when(j == 0)
    def _():
        _flash_reset(m_ref, l_ref, acc_ref)

    Q = q_ref[0]
    raw = jnp.concatenate([pages[p][0, 0].astype(BF16) for p in range(P)], axis=0)
    s = _dot_nt(Q, raw)
    slot = lax.broadcasted_iota(I32, s.shape, 1) & (SLOTS - 1)
    grp = lax.broadcasted_iota(I32, s.shape, 0) >> 2
    ok = slot == grp
    m_ref[...], l_ref[...], acc_ref[...] = _slot_update(
        (m_ref[...], l_ref[...], acc_ref[...]), jnp.where(ok, s, NEG), ok, raw, DIFF_KV_HEADS)

    @pl.when(j == pl.num_programs(1) - 1)
    def _():
        s_new = jnp.sum(Q.astype(F32) * kn_ref[0].astype(F32), axis=-1, keepdims=True)
        a = _with_new(m_ref[...], l_ref[...], acc_ref[...], s_new, vn_ref[0].astype(F32))
        lam = _lambda(lam_ref, lam_init)
        g = gn_ref[...]
        outs = []
        for h in range(DIFF_HEADS):
            o = a[2 * h:2 * h + 1] - lam * a[2 * h + 1:2 * h + 2]
            outs.append(_head_norm(o, g, 1.0 - lam_init))
        o_ref[0] = jnp.concatenate(outs, axis=0).astype(o_ref.dtype)


def diff_sample(q16, kn16, vn16, cache, page_table_flat, lam_param, out_norm, lam_init, *, layer, batch, n_pages):
    P = 8
    R = PAGE_SIZE * SLOTS
    page = lambda p: pl.BlockSpec((1, 1, R, LANES), lambda b, j, pt: (layer, pt[b * n_pages + j * P + p], 0, 0))
    per_b = pl.BlockSpec((1, 16, LANES), lambda b, j, pt: (b, 0, 0))
    gs = pltpu.PrefetchScalarGridSpec(
        num_scalar_prefetch=1, grid=(batch, n_pages // P),
        in_specs=[pl.BlockSpec((4, DIFF_DH), lambda b, j, pt: (0, 0)),
                  pl.BlockSpec((1, LANES), lambda b, j, pt: (0, 0)),
                  per_b, per_b, per_b] + [page(p) for p in range(P)],
        out_specs=pl.BlockSpec((1, DIFF_HEADS, LANES), lambda b, j, pt: (b, 0, 0)),
        scratch_shapes=[pltpu.VMEM((16, 1), F32), pltpu.VMEM((16, 1), F32), pltpu.VMEM((16, LANES), F32)])
    return pl.pallas_call(
        functools.partial(_diff_sample_kernel, P=P, lam_init=lam_init), grid_spec=gs,
        out_shape=jax.ShapeDtypeStruct((batch, DIFF_HEADS, LANES), BF16),
        compiler_params=_cparams("parallel", "arbitrary"), name="diff_sample",
    )(page_table_flat, lam_param, out_norm.reshape(1, LANES), q16, kn16, vn16, *([cache] * P))


def _moba_gate_kernel(pt_ref, q_ref, *refs, nblk, bps):
    pages = refs[:2 * bps]
    o_ref, gate_ref = refs[2 * bps:]
    n = pl.program_id(1)
    lane = lax.broadcasted_iota(I32, (MOBA_HEADS, LANES), 1)

    @pl.when(n == 0)
    def _():
        gate_ref[...] = jnp.full(gate_ref.shape, NEG, F32)

    q = q_ref[0].astype(F32)
    gate = gate_ref[...]
    for t in range(bps):
        ksum = (jnp.sum(pages[2 * t][0, 0].reshape(PAGE_SIZE, SLOTS, LANES), axis=0)
                + jnp.sum(pages[2 * t + 1][0, 0].reshape(PAGE_SIZE, SLOTS, LANES), axis=0))
        kmean = (ksum * (1.0 / MOBA_BLOCK)).astype(BF16).astype(F32)
        per_head = jnp.concatenate([kmean[h // 2:h // 2 + 1] for h in range(MOBA_HEADS)], axis=0)
        g = jnp.sum(q * per_head, axis=-1, keepdims=True)
        gate = jnp.where(lane == n * bps + t, g, gate)
    gate_ref[...] = gate

    @pl.when(n == nblk // bps - 1)
    def _():
        cnt = _rank_lt(gate, nblk, lane)
        lanef = lane.astype(F32)
        out = jnp.zeros(gate.shape, F32)
        for k in range(MOBA_TOPK):
            hit = cnt == float(k)
            idx = jnp.sum(jnp.where(hit, lanef, 0.0), axis=-1, keepdims=True)
            val = jnp.max(jnp.where(hit, gate, NEG), axis=-1, keepdims=True)
            out = out + jnp.where(lane == k, idx, 0.0) + jnp.where((lane == 4 + k) & (val > NEG / 2), 1.0, 0.0)
        o_ref[0] = out.astype(I32)


def moba_gate_sample(q, cache, page_table_flat, *, layer, batch, n_pages):
    nblk = n_pages * PAGE_SIZE // MOBA_BLOCK
    ppb = MOBA_BLOCK // PAGE_SIZE
    bps = 4 if nblk % 4 == 0 else 1
    assert ppb == 2 and nblk <= LANES
    R = PAGE_SIZE * SLOTS
    page = lambda p: pl.BlockSpec((1, 1, R, LANES),
                                  lambda b, n, pt: (layer, pt[b * n_pages + n * bps * ppb + p], 0, 0))
    gs = pltpu.PrefetchScalarGridSpec(
        num_scalar_prefetch=1, grid=(batch, nblk // bps),
        in_specs=[pl.BlockSpec((1, MOBA_HEADS, LANES), lambda b, n, pt: (b, 0, 0))]
        + [page(p) for p in range(bps * ppb)],
        out_specs=pl.BlockSpec((1, MOBA_HEADS, LANES), lambda b, n, pt: (b, 0, 0)),
        scratch_shapes=[pltpu.VMEM((MOBA_HEADS, LANES), F32)])
    return pl.pallas_call(
        functools.partial(_moba_gate_kernel, nblk=nblk, bps=bps), grid_spec=gs,
        out_shape=jax.ShapeDtypeStruct((batch, MOBA_HEADS, LANES), I32),
        compiler_params=_cparams("parallel", "arbitrary"), name="moba_gate_sample",
    )(page_table_flat, q, *([cache] * (bps * ppb)))


def _moba_sample_kernel(pg_ref, ok_ref, q_ref, kn_ref, vn_ref, *refs, npage):
    pages, o_ref = refs[:npage], refs[npage]
    b, h = pl.program_id(0), pl.program_id(1)
    Q = q_ref[0, 0]
    carry = _init_carry(8, LANES)
    for t in range(npage):
        raw = pages[t][0, 0].astype(BF16)
        on = ok_ref[(b * MOBA_HEADS + h) * MOBA_TOPK + t // (npage // MOBA_TOPK)] > 0
        s, is_key = _slot_scores(Q, raw, h // 2, SLOTS)
        ok = is_key & on
        carry = _slot_update(carry, jnp.where(ok, s, NEG), ok, raw, MOBA_KV_HEADS)
    s_new = jnp.sum(Q.astype(F32) * kn_ref[0].astype(F32), axis=-1, keepdims=True)
    o = _with_new(*carry, s_new, vn_ref[0].astype(F32))
    o_ref[0] = o[0:1].astype(o_ref.dtype)


def moba_sample(q8, kv_new, cache, sel_pages, sel_ok, *, layer, batch):
    npage = MOBA_TOPK * (MOBA_BLOCK // PAGE_SIZE)
    G = MOBA_KV_HEADS
    R = PAGE_SIZE * SLOTS
    page = lambda t: pl.BlockSpec(
        (1, 1, R, LANES), lambda b, h, pgs, oks: (layer, pgs[(b * MOBA_HEADS + h) * npage + t], 0, 0))
    gs = pltpu.PrefetchScalarGridSpec(
        num_scalar_prefetch=2, grid=(batch, MOBA_HEADS),
        in_specs=[pl.BlockSpec((1, 1, 8, LANES), lambda b, h, pgs, oks: (b, h, 0, 0)),
                  pl.BlockSpec((1, 1, LANES), lambda b, h, pgs, oks: (b, 0, h // 2)),
                  pl.BlockSpec((1, 1, LANES), lambda b, h, pgs, oks: (b, 0, G + h // 2))]
        + [page(t) for t in range(npage)],
        out_specs=pl.BlockSpec((1, 1, LANES), lambda b, h, pgs, oks: (b, 0, h)))
    return pl.pallas_call(
        functools.partial(_moba_sample_kernel, npage=npage), grid_spec=gs,
        out_shape=jax.ShapeDtypeStruct((batch, 1, MOBA_HEADS * LANES), BF16),
        compiler_params=_cparams("parallel", "parallel"), name="moba_sample",
    )(sel_pages, sel_ok, q8, kv_new, kv_new, *([cache] * npage))


def _nsa_chunk_kernel(pt_ref, *refs, P):
    pages, o_ref = refs[:P], refs[P]
    n = PAGE_SIZE // CMP_STRIDE
    for p in range(P):
        for cg in range(4):
            for j in range(CMP_STRIDE):
                o_ref[0, cg, p * n:(p + 1) * n, j * LANES:(j + 1) * LANES] = (
                    pages[p][0, 0, pl.ds(j * SLOTS + cg, n, stride=CMP_STRIDE * SLOTS), :])


def nsa_chunk_sample(cache, page_table_flat, *, layer, batch, n_pages):
    n = PAGE_SIZE // CMP_STRIDE
    P = 4
    R = PAGE_SIZE * SLOTS
    page = lambda p: pl.BlockSpec((1, 1, R, LANES), lambda b, s, pt: (layer, pt[b * n_pages + s * P + p], 0, 0))
    gs = pltpu.PrefetchScalarGridSpec(
        num_scalar_prefetch=1, grid=(batch, n_pages // P),
        in_specs=[page(p) for p in range(P)],
        out_specs=pl.BlockSpec((1, 4, P * n, CMP_STRIDE * LANES), lambda b, s, pt: (b, 0, s, 0)))
    return pl.pallas_call(
        functools.partial(_nsa_chunk_kernel, P=P), grid_spec=gs,
        out_shape=jax.ShapeDtypeStruct((batch, 4, n_pages * n, CMP_STRIDE * LANES), F32),
        compiler_params=_cparams("parallel", "parallel"), name="nsa_chunk_sample",
    )(page_table_flat, *([cache] * P))


def _nsa_cmp_sample_kernel(q_ref, ck_ref, cv_ref, imp_ref, oc_ref, sel_ref, *, n_tok, n_valid, cur, n_blk_pad):
    Q = q_ref[0, 0]
    lane = lax.broadcasted_iota(I32, (NSA_REP, n_tok), 1)
    ok = lane < n_valid
    s = jnp.where(ok, _dot_nt(Q, ck_ref[0, 0]), NEG)
    p = jnp.where(ok, jnp.exp(s - jnp.max(s, axis=-1, keepdims=True)), 0.0)
    pc = p / jnp.maximum(jnp.sum(p, axis=-1, keepdims=True), TINY)
    oc_ref[0, 0] = _dot(pc.astype(BF16), cv_ref[0, 0])
    pcs = jnp.broadcast_to(jnp.sum(pc, axis=0, keepdims=True), (8, n_tok))
    imp = _dot_exact(pcs, imp_ref[...])[0:1]
    blk = lax.broadcasted_iota(I32, (1, n_blk_pad), 1)
    valid = blk <= cur
    forced = (blk == 0) | (blk == cur) | (blk == cur - 1)
    imp = jnp.where(valid, jnp.where(forced, FORCE_SCORE, imp), NEG)
    rows = jnp.broadcast_to(imp, (n_blk_pad, n_blk_pad))
    cols = rows.T
    mi = lax.broadcasted_iota(I32, (n_blk_pad, n_blk_pad), 0)
    ji = lax.broadcasted_iota(I32, (n_blk_pad, n_blk_pad), 1)
    beats = (cols > rows) | ((cols == rows) & (mi < ji))
    cnt = jnp.sum(jnp.where(beats, 1.0, 0.0), axis=0, keepdims=True)
    blkf = blk.astype(F32)
    lane_o = lax.broadcasted_iota(I32, (1, LANES), 1)
    out = jnp.zeros((1, LANES), F32)
    for k in range(SEL_TOPK):
        hit = cnt == float(k)
        idx = jnp.sum(jnp.where(hit, blkf, 0.0), axis=-1, keepdims=True)
        val = jnp.max(jnp.where(hit, imp, NEG), axis=-1, keepdims=True)
        out = out + jnp.where(lane_o == k, idx, 0.0) + jnp.where((lane_o == SEL_TOPK + k) & (val > NEG / 2), 1.0, 0.0)
    sel_ref[0, 0] = jnp.broadcast_to(out, (8, LANES)).astype(I32)


def nsa_cmp_sample(q8, cmp_tok, *, batch, past_len):
    G = NSA_KV_HEADS
    n_tok = cmp_tok.shape[2]
    n_valid = n_tok - 1
    cur = past_len // SEL_BLOCK
    n_blk = cur + 1
    n_blk_pad = -(-n_blk // LANES) * LANES
    imp_m = _importance_matrix(n_tok, n_tok, n_blk_pad, n_blk)
    return pl.pallas_call(
        functools.partial(_nsa_cmp_sample_kernel, n_tok=n_tok, n_valid=n_valid, cur=cur, n_blk_pad=n_blk_pad),
        grid=(batch, G),
        in_specs=[pl.BlockSpec((1, 1, NSA_REP, LANES), lambda b, g: (b, g, 0, 0)),
                  pl.BlockSpec((1, 1, n_tok, LANES), lambda b, g: (b, g, 0, 0)),
                  pl.BlockSpec((1, 1, n_tok, LANES), lambda b, g: (b, G + g, 0, 0)),
                  pl.BlockSpec((n_tok, n_blk_pad), lambda b, g: (0, 0))],
        out_specs=[pl.BlockSpec((1, 1, NSA_REP, LANES), lambda b, g: (b, g, 0, 0)),
                   pl.BlockSpec((1, 1, 8, LANES), lambda b, g: (b, g, 0, 0))],
        out_shape=[jax.ShapeDtypeStruct((batch, G, NSA_REP, LANES), F32),
                   jax.ShapeDtypeStruct((batch, G, 8, LANES), I32)],
        compiler_params=_cparams("parallel", "parallel"), name="nsa_cmp_sample",
    )(q8, cmp_tok, cmp_tok, imp_m)


def _nsa_sel_sample_kernel(pg_ref, hb_ref, ok_ref, own_ref, q_ref, gt_ref, oc_ref, kn_ref, vn_ref, wkn_ref,
                           wvn_ref, win_ref, *refs):
    blks, o_ref = refs[:SEL_TOPK], refs[SEL_TOPK]
    b, g = pl.program_id(0), pl.program_id(1)
    G = NSA_KV_HEADS
    Q = q_ref[0, 0]
    carry = _init_carry(NSA_REP, LANES)
    for t in range(SEL_TOPK):
        raw = blks[t][0, 0].astype(BF16)
        on = ok_ref[(b * G + g) * SEL_TOPK + t] > 0
        s, is_key = _slot_scores(Q, raw, 2 * G + g, SLOTS)
        ok = is_key & on
        carry = _slot_update(carry, jnp.where(ok, s, NEG), ok, raw, G)
    Qf = Q.astype(F32)

    def new_score(kn_ref):
        return jnp.sum(Qf * kn_ref[0].astype(F32), axis=-1, keepdims=True)

    own = own_ref[b * G + g] > 0
    o_s = _with_new(*carry, new_score(kn_ref), vn_ref[0].astype(F32), own)
    raww = win_ref[0, 0].astype(BF16)
    sw, okw = _slot_scores(Q, raww, g, 2 * G)
    cw = _slot_update(_init_carry(NSA_REP, LANES), jnp.where(okw, sw, NEG), okw, raww, G)
    o_w = _with_new(*cw, new_score(wkn_ref), wvn_ref[0].astype(F32))
    gt = gt_ref[0, 0]
    o_ref[0, 0] = gt[:, 0:1] * oc_ref[0, 0] + gt[:, 1:2] * o_s + gt[:, 2:3] * o_w


def nsa_sel_sample(q8, gates8, o_c, rows_new, win_new, state_win, cache, sel_pages, sel_half, sel_ok, sel_own, *,
                   layer, batch):
    G = NSA_KV_HEADS
    wrows = state_win.shape[2]
    idx = lambda b, g, t: (b * G + g) * SEL_TOPK + t
    new = lambda c: pl.BlockSpec((1, 1, LANES), lambda b, g, pgs, hbs, oks, own: (b, 0, c + g))
    per_bg = pl.BlockSpec((1, 1, NSA_REP, LANES), lambda b, g, pgs, hbs, oks, own: (b, g, 0, 0))
    blk = lambda t: pl.BlockSpec(
        (1, 1, SEL_BLOCK * SLOTS, LANES),
        lambda b, g, pgs, hbs, oks, own: (layer, pgs[idx(b, g, t)], hbs[idx(b, g, t)], 0))
    gs = pltpu.PrefetchScalarGridSpec(
        num_scalar_prefetch=4, grid=(batch, G),
        in_specs=[per_bg, per_bg, per_bg, new(4), new(6), new(0), new(2),
                  pl.BlockSpec((1, 1, wrows, LANES), lambda b, g, pgs, hbs, oks, own: (layer, b, 0, 0))]
        + [blk(t) for t in range(SEL_TOPK)],
        out_specs=per_bg)
    return pl.pallas_call(
        _nsa_sel_sample_kernel, grid_spec=gs,
        out_shape=jax.ShapeDtypeStruct((batch, G, NSA_REP, LANES), F32),
        compiler_params=_cparams("parallel", "parallel"), name="nsa_sel_sample",
    )(sel_pages, sel_half, sel_ok, sel_own, q8, gates8, o_c, rows_new, rows_new, win_new, win_new,
      state_win, *([cache] * SEL_TOPK))


def _prep_weights(w_in, ffn1_w_down, ffn2_w_down, w_branch, w_out, nsa_cmp_w1, nsa_cmp_w2, nsa_cmp_pos):
    depth = w_in.shape[0]
    bf = lambda w: w.astype(BF16)
    src = np.zeros((NSA_KV_HEADS, 3, NSA_REP), np.int32)
    for g in range(NSA_KV_HEADS):
        for t in range(3):
            for r in range(NSA_REP):
                src[g, t, r] = (g * NSA_REP + r) * 3 + t
    w_ng = w_in[:, :, QKV_COLS:QKV_COLS + NG_COLS][:, :, src.reshape(NSA_KV_HEADS, 3 * NSA_REP)]
    w_ng = jnp.pad(w_ng, ((0, 0), (0, 0), (0, 0), (0, LANES - 3 * NSA_REP))).reshape(depth, D_MODEL, NG_PAD)
    W = CMP_STRIDE * HEAD_DIM
    return dict(
        down1=bf(ffn1_w_down), down2=bf(ffn2_w_down), qkv=bf(w_in[:, :, :QKV_COLS]), ng=bf(w_ng),
        mg=bf(w_in[:, :, QKV_COLS + NG_COLS:]),
        branch=bf(w_branch), out=bf(w_out),
        cw0=bf(nsa_cmp_w1[:, :, :CMP_STRIDE].reshape(depth, 2, W, CMP_HIDDEN)),
        cw1=bf(nsa_cmp_w1[:, :, CMP_STRIDE:].reshape(depth, 2, W, CMP_HIDDEN)),
        cw2=bf(nsa_cmp_w2),
        cpe=nsa_cmp_pos.reshape(depth, 2, 2, W),
    )


def _ffn(x, norm_g, w_gu, w_down, layer):
    h = rmsnorm_bf16(x, norm_g)
    act = matmul_swiglu(h, w_gu, layer)
    return matmul_residual(act, w_down, layer, x, 0.5, tm=512, tn=256, name="ffn_down")


def _project(x, mix_g, wp, layer, tables, gd, gm, gn, *, seq, batch):
    h = rmsnorm_bf16(x, mix_g)
    proj = matmul(h, wp['qkv'], layer, n_cols=QKV_COLS, tn=512, name="in_proj_qkv")
    ng = matmul(h, wp['ng'], layer, n_cols=NG_PAD, tn=NG_PAD, sigmoid=True, name="in_proj_gate")
    mg = matmul(h, wp['mg'], layer, n_cols=MG_COLS, tn=512, sigmoid=True, name="in_proj_merge_gate")
    post = post_project(proj, tables, gd, gm, gn, seq=seq, batch=batch)
    return post, ng, mg


def _merge(x, o_d, o_m, o_n, mg, wp, layer):
    u = matmul_merge(o_d, o_m, o_n, wp['branch'], layer, mg)
    return matmul_residual(u, wp['out'], layer, x, 1.0, tm=1024, tn=512, name="out_proj")


def kernel(x_prompt, x_sample, cache_diff, cache_moba, cache_nsa, state_nsa_win, page_table, ffn1_norm, ffn1_w_gu, ffn1_w_down, mix_norm, w_in, diff_qk_norm, diff_lambda, diff_out_norm, moba_qk_norm, nsa_qk_norm, nsa_cmp_w1, nsa_cmp_w2, nsa_cmp_pos, w_branch, w_out, ffn2_norm, ffn2_w_gu, ffn2_w_down):
    B, T, D = x_prompt.shape
    Bs = x_sample.shape[0]
    depth = ffn1_norm.shape[0]
    n_pool = cache_diff.shape[1]
    n_pages = page_table.shape[1]
    past_len = n_pages * PAGE_SIZE
    MS = 16
    assert x_sample.shape[1] == 1 and Bs <= MS and T % MOBA_BLOCK == 0

    xp = x_prompt.reshape(B * T, D)
    xs = jnp.pad(x_sample.reshape(Bs, D), ((0, MS - Bs), (0, 0)))
    c_diff = cache_diff.reshape(depth, n_pool, PAGE_SIZE * SLOTS, HEAD_DIM)
    c_moba = cache_moba.reshape(depth, n_pool, PAGE_SIZE * SLOTS, HEAD_DIM)
    c_nsa = cache_nsa.reshape(depth, n_pool, PAGE_SIZE * SLOTS, HEAD_DIM)
    wlen = state_nsa_win.shape[2]
    s_win = state_nsa_win.reshape(depth, Bs, wlen * 2 * NSA_KV_HEADS, HEAD_DIM)
    pt_flat = page_table.reshape(-1).astype(I32)
    tab_p = _rope_tables(jnp.arange(T, dtype=I32))
    tab_s = _rope_tables(jnp.full((MS,), past_len, I32))
    wp = _prep_weights(w_in, ffn1_w_down, ffn2_w_down, w_branch, w_out, nsa_cmp_w1, nsa_cmp_w2, nsa_cmp_pos)

    outs = [[] for _ in range(8)]
    for l in range(depth):
        lam_init = 0.8 - 0.6 * math.exp(-0.3 * l)
        gd = jnp.tile(diff_qk_norm[l], (1, 2))
        gm, gn = moba_qk_norm[l], nsa_qk_norm[l]
        lam_p = diff_lambda[l]
        cmp_w = (wp['cpe'][l], wp['cw0'][l], wp['cw1'][l], wp['cw2'][l])

        xp = _ffn(xp, ffn1_norm[l], ffn1_w_gu, wp['down1'], l)
        xs = _ffn(xs, ffn1_norm[l], ffn1_w_gu, wp['down1'], l)

        post, ng, mg = _project(xp, mix_norm[l], wp, l, tab_p, gd, gm, gn, seq=T, batch=B)
        dq, rd, rdb, mq, rm, rmb, nq, rn, rnb, win, winb, xc = post
        o_d = diff_prompt(dq, rdb, lam_p, diff_out_norm[l], lam_init, batch=B, seq=T)
        o_m = moba_prompt(mq, rm, rmb, batch=B, seq=T)
        cmp_tok = compress(xc, *cmp_w)
        o_n = nsa_prompt(nq, ng, cmp_tok, rnb, winb, batch=B, seq=T)
        xp = _merge(xp, o_d, o_m, o_n, mg, wp, l)
        outs[0].append(rd.reshape(B, T, 2, DIFF_KV_HEADS, HEAD_DIM))
        outs[2].append(rm.reshape(B, T, 2, MOBA_KV_HEADS, HEAD_DIM))
        outs[4].append(rn.reshape(B, T, 4, NSA_KV_HEADS, HEAD_DIM))
        wkeep = min(WINDOW, T)
        outs[6].append(win.reshape(B, T, 2, NSA_KV_HEADS, HEAD_DIM)[:, T - wkeep:])

        post, ng, mg = _project(xs, mix_norm[l], wp, l, tab_s, gd, gm, gn, seq=1, batch=Bs)
        dq, rd, rdb, mq, rm, rmb, nq, rn, rnb, win, winb = post
        q5 = dq[:Bs].reshape(Bs, DIFF_KV_HEADS, 2, 2, DIFF_DH)
        eye_c = jnp.eye(2, dtype=BF16)
        q16 = (q5[:, :, :, :, None, :] * eye_c[None, None, None, :, :, None]).reshape(Bs, 16, LANES)
        kv4 = rdb[:Bs].reshape(Bs, 2, DIFF_KV_HEADS, 1, LANES)
        kvn16 = jnp.broadcast_to(kv4, (Bs, 2, DIFF_KV_HEADS, 4, LANES)).reshape(Bs, 2, 16, LANES)
        o_d = diff_sample(q16, kvn16[:, 0], kvn16[:, 1], c_diff, pt_flat, lam_p, diff_out_norm[l], lam_init,
                          layer=l, batch=Bs, n_pages=n_pages)
        msel = moba_gate_sample(mq[:Bs].reshape(Bs, MOBA_HEADS, LANES), c_moba, pt_flat,
                                layer=l, batch=Bs, n_pages=n_pages)
        m_idx = msel[:, :, 0:MOBA_TOPK]
        m_ok = msel[:, :, 4:4 + MOBA_TOPK]
        ppb = MOBA_BLOCK // PAGE_SIZE
        logical = (m_idx[..., None] * ppb + jnp.arange(ppb, dtype=I32)).reshape(Bs, MOBA_HEADS * MOBA_TOPK * ppb)
        m_pages = jnp.take_along_axis(page_table.astype(I32), logical, axis=1)
        q8 = jnp.broadcast_to(mq[:Bs].reshape(Bs, MOBA_HEADS, 1, HEAD_DIM), (Bs, MOBA_HEADS, 8, HEAD_DIM))
        o_m = moba_sample(q8, rmb[:Bs].reshape(Bs, 1, 1024), c_moba, m_pages.reshape(-1), m_ok.reshape(-1),
                          layer=l, batch=Bs)
        xcs = nsa_chunk_sample(c_nsa, pt_flat, layer=l, batch=Bs, n_pages=n_pages)
        cmp_tok = compress(xcs, *cmp_w)
        nq8 = nq[:Bs].reshape(Bs, NSA_KV_HEADS, NSA_REP, HEAD_DIM)
        o_c, nsel = nsa_cmp_sample(nq8, cmp_tok, batch=Bs, past_len=past_len)
        n_idx = nsel[:, :, 0, 0:SEL_TOPK]
        n_ok = nsel[:, :, 0, SEL_TOPK:2 * SEL_TOPK]
        bpp = PAGE_SIZE // SEL_BLOCK
        in_past = n_idx < past_len // SEL_BLOCK
        n_log = jnp.where(in_past, n_idx // bpp, 0)
        n_pages_sel = jnp.take_along_axis(page_table.astype(I32), n_log.reshape(Bs, -1), axis=1)
        n_half = jnp.where(in_past, n_idx % bpp, 0)
        n_on = (n_ok > 0) & in_past
        n_own = jnp.any((n_ok > 0) & (n_idx == past_len // SEL_BLOCK), axis=-1)
        gates8 = jnp.pad(ng[:Bs].reshape(Bs, NSA_KV_HEADS, LANES)[:, :, :3 * NSA_REP]
                         .reshape(Bs, NSA_KV_HEADS, 3, NSA_REP).transpose(0, 1, 3, 2),
                         ((0, 0), (0, 0), (0, 0), (0, LANES - 3)))
        o_n = nsa_sel_sample(nq8, gates8, o_c, rnb[:Bs].reshape(Bs, 1, 1024), winb[:Bs].reshape(Bs, 1, 512),
                             s_win, c_nsa, n_pages_sel.reshape(-1), n_half.reshape(-1).astype(I32),
                             n_on.reshape(-1).astype(I32), n_own.reshape(-1).astype(I32), layer=l, batch=Bs)
        pad_rows = lambda a, w: jnp.pad(a.reshape(Bs, w).astype(BF16), ((0, MS - Bs), (0, 0)))
        xs = _merge(xs, pad_rows(o_d, 1024), pad_rows(o_m, 1024), pad_rows(o_n, 2048), mg, wp, l)
        outs[1].append(rd[:Bs].reshape(Bs, 1, 2, DIFF_KV_HEADS, HEAD_DIM))
        outs[3].append(rm[:Bs].reshape(Bs, 1, 2, MOBA_KV_HEADS, HEAD_DIM))
        outs[5].append(rn[:Bs].reshape(Bs, 1, 4, NSA_KV_HEADS, HEAD_DIM))
        new_win = win[:Bs].reshape(Bs, 1, 2, NSA_KV_HEADS, HEAD_DIM)
        outs[7].append(jnp.concatenate([state_nsa_win[l], new_win], axis=1)[:, 1:])

        xp = _ffn(xp, ffn2_norm[l], ffn2_w_gu, wp['down2'], l)
        xs = _ffn(xs, ffn2_norm[l], ffn2_w_gu, wp['down2'], l)

    return (xp.reshape(B, T, D), xs[:Bs].reshape(Bs, 1, D)) + tuple(jnp.stack(o) for o in outs)
```

```python
import functools
import math

import jax
import jax.numpy as jnp
import numpy as np
from jax import lax
from jax.experimental import pallas as pl
from jax.experimental.pallas import tpu as pltpu

F32 = jnp.float32
BF16 = jnp.bfloat16
I32 = jnp.int32

D_MODEL = 4096
D_FF = 11008
HEAD_DIM = 128
PAGE_SIZE = 128
ROPE_THETA = 10000.0
EPS = 1e-6
NEG = -1e30
TINY = 1e-30
DIFF_HEADS, DIFF_KV_HEADS, DIFF_DH = 8, 4, 64
MOBA_HEADS, MOBA_KV_HEADS, MOBA_BLOCK, MOBA_TOPK = 8, 4, 256, 3
NSA_HEADS, NSA_KV_HEADS = 16, 2
NSA_REP = NSA_HEADS // NSA_KV_HEADS
CMP_BLOCK, CMP_STRIDE, CMP_HIDDEN = 32, 16, 256
SEL_BLOCK, SEL_TOPK, WINDOW = 64, 16, 512
FORCE_SCORE = 1e4
QKV_COLS = 7680
NG_COLS = NSA_HEADS * 3
NG_PAD = NSA_KV_HEADS * 128
MG_COLS = 3 * D_MODEL

LANES = 128
SUBLANES = 8
VMEM_LIMIT_BYTES = 56 * 1024 * 1024
SCALE128 = HEAD_DIM ** -0.5
SLOTS = 8


def _cparams(*sem):
    return pltpu.CompilerParams(dimension_semantics=sem, vmem_limit_bytes=VMEM_LIMIT_BYTES)


def _dot(a, b):
    return jnp.dot(a, b, preferred_element_type=F32)


def _dot_nt(a, b):
    return lax.dot_general(a, b, (((1,), (1,)), ((), ())), preferred_element_type=F32)


def _dot_exact(a, b):
    return jnp.dot(a, b, preferred_element_type=F32, precision=lax.Precision.HIGHEST)


def _rmsnorm_kernel(x_ref, g_ref, o_ref):
    x = x_ref[...]
    ms = jnp.mean(x * x, axis=-1, keepdims=True)
    o_ref[...] = (x * lax.rsqrt(ms + EPS) * g_ref[...]).astype(o_ref.dtype)


def rmsnorm_bf16(x, g):
    M, D = x.shape
    tm = min(M, 256)
    return pl.pallas_call(
        _rmsnorm_kernel, grid=(M // tm,),
        in_specs=[pl.BlockSpec((tm, D), lambda i: (i, 0)), pl.BlockSpec((1, D), lambda i: (0, 0))],
        out_specs=pl.BlockSpec((tm, D), lambda i: (i, 0)),
        out_shape=jax.ShapeDtypeStruct((M, D), BF16),
        compiler_params=_cparams("parallel"), name="rmsnorm",
    )(x, g.reshape(1, D))


def _mm_kernel(a_ref, b_ref, o_ref, *, sigmoid):
    acc = _dot(a_ref[...], b_ref[...].astype(BF16))
    if sigmoid:
        acc = jax.nn.sigmoid(acc)
    o_ref[...] = acc.astype(o_ref.dtype)


def matmul(a, w, layer, *, n_cols, tn, out_dtype=F32, sigmoid=False, name="matmul"):
    M, K = a.shape
    tm = min(M, 1024)
    return pl.pallas_call(
        functools.partial(_mm_kernel, sigmoid=sigmoid), grid=(M // tm, n_cols // tn),
        in_specs=[pl.BlockSpec((tm, K), lambda i, j: (i, 0)),
                  pl.BlockSpec((None, K, tn), lambda i, j: (layer, 0, j))],
        out_specs=pl.BlockSpec((tm, tn), lambda i, j: (i, j)),
        out_shape=jax.ShapeDtypeStruct((M, n_cols), out_dtype),
        compiler_params=_cparams("parallel", "parallel"), name=name,
    )(a, w)


def _mm_swiglu_kernel(a_ref, bg_ref, bu_ref, o_ref):
    a = a_ref[...]
    g = _dot(a, bg_ref[...].astype(BF16))
    u = _dot(a, bu_ref[...].astype(BF16))
    o_ref[...] = (g * jax.nn.sigmoid(g) * u).astype(o_ref.dtype)


def matmul_swiglu(a, w_gu, layer, *, tn=256):
    M, K = a.shape
    F = w_gu.shape[2] // 2
    tm = min(M, 1024)
    nf = F // tn
    return pl.pallas_call(
        _mm_swiglu_kernel, grid=(M // tm, nf),
        in_specs=[pl.BlockSpec((tm, K), lambda i, j: (i, 0)),
                  pl.BlockSpec((None, K, tn), lambda i, j: (layer, 0, j)),
                  pl.BlockSpec((None, K, tn), lambda i, j: (layer, 0, j + nf))],
        out_specs=pl.BlockSpec((tm, tn), lambda i, j: (i, j)),
        out_shape=jax.ShapeDtypeStruct((M, F), BF16),
        compiler_params=_cparams("parallel", "parallel"), name="ffn_gate_up",
    )(a, w_gu, w_gu)


def _mm_res_kernel(a_ref, b_ref, x_ref, o_ref, *, alpha):
    o_ref[...] = x_ref[...] + alpha * _dot(a_ref[...], b_ref[...].astype(BF16))


def matmul_residual(a, w, layer, x, alpha, *, tm, tn, name):
    M, K = a.shape
    N = w.shape[2]
    tm = min(M, tm)
    return pl.pallas_call(
        functools.partial(_mm_res_kernel, alpha=alpha), grid=(M // tm, N // tn),
        in_specs=[pl.BlockSpec((tm, K), lambda i, j: (i, 0)),
                  pl.BlockSpec((None, K, tn), lambda i, j: (layer, 0, j)),
                  pl.BlockSpec((tm, tn), lambda i, j: (i, j))],
        out_specs=pl.BlockSpec((tm, tn), lambda i, j: (i, j)),
        out_shape=jax.ShapeDtypeStruct((M, N), F32),
        compiler_params=_cparams("parallel", "parallel"), name=name,
    )(a, w, x)


def _mm_merge_kernel(od_ref, om_ref, on_ref, wd_ref, wm_ref, wn_ref, g0_ref, g1_ref, g2_ref, o_ref):
    u = g0_ref[...] * _dot(od_ref[...], wd_ref[...])
    u = u + g1_ref[...] * _dot(om_ref[...], wm_ref[...])
    u = u + g2_ref[...] * _dot(on_ref[...], wn_ref[...])
    o_ref[...] = u.astype(o_ref.dtype)


def matmul_merge(o_d, o_m, o_n, w_branch, layer, mg, *, tn=512):
    M = o_d.shape[0]
    D = w_branch.shape[2]
    tm = min(M, 1024)
    wd, wm, wn = o_d.shape[1], o_m.shape[1], o_n.shape[1]
    nd = D // tn
    return pl.pallas_call(
        _mm_merge_kernel, grid=(M // tm, nd),
        in_specs=[pl.BlockSpec((tm, wd), lambda i, j: (i, 0)),
                  pl.BlockSpec((tm, wm), lambda i, j: (i, 0)),
                  pl.BlockSpec((tm, wn), lambda i, j: (i, 0)),
                  pl.BlockSpec((None, wd, tn), lambda i, j: (layer, 0, j)),
                  pl.BlockSpec((None, wm, tn), lambda i, j: (layer, 1, j)),
                  pl.BlockSpec((None, wn, tn), lambda i, j: (layer, 1, j)),
                  pl.BlockSpec((tm, tn), lambda i, j: (i, j)),
                  pl.BlockSpec((tm, tn), lambda i, j: (i, j + nd)),
                  pl.BlockSpec((tm, tn), lambda i, j: (i, j + 2 * nd))],
        out_specs=pl.BlockSpec((tm, tn), lambda i, j: (i, j)),
        out_shape=jax.ShapeDtypeStruct((M, D), BF16),
        compiler_params=_cparams("parallel", "parallel"), name="merge",
    )(o_d, o_m, o_n, w_branch, w_branch, w_branch, mg, mg, mg)


def _post_kernel(p_ref, c128_ref, s128_ref, c64_ref, s64_ref, gd_ref, gm_ref, gn_ref,
                 dq_ref, rd_ref, rdb_ref, mq_ref, rm_ref, rmb_ref, nq_ref, rn_ref, rnb_ref,
                 win_ref, winb_ref, *rest, tm, emit_chunks):
    lane = lax.broadcasted_iota(I32, (tm, LANES), 1)
    lo = lane < 64
    first_half64 = (lane & 63) < 32
    c128, s128 = c128_ref[...], s128_ref[...]
    c64, s64 = c64_ref[...], s64_ref[...]

    def slab(k):
        return p_ref[:, k * LANES:(k + 1) * LANES]

    def nr128(x, g):
        ms = jnp.mean(x * x, axis=-1, keepdims=True)
        y = x * lax.rsqrt(ms + EPS) * g
        return y * c128 + pltpu.roll(y, 64, 1) * s128

    def nr64(x, g):
        x2 = x * x
        s_lo = jnp.sum(jnp.where(lo, x2, 0.0), axis=-1, keepdims=True)
        s_hi = jnp.sum(jnp.where(lo, 0.0, x2), axis=-1, keepdims=True)
        ms = jnp.where(lo, s_lo, s_hi) * (1.0 / 64.0)
        y = x * lax.rsqrt(ms + EPS) * g
        partner = jnp.where(first_half64, pltpu.roll(y, 96, 1), pltpu.roll(y, 32, 1))
        return y * c64 + partner * s64

    def put(ref_f, ref_b, k, val):
        if ref_f is not None:
            ref_f[:, k * LANES:(k + 1) * LANES] = val
        if ref_b is not None:
            ref_b[:, k * LANES:(k + 1) * LANES] = val.astype(BF16)

    gdq, gdk = gd_ref[0:1, :], gd_ref[1:2, :]
    gmq, gmk = gm_ref[0:1, :], gm_ref[1:2, :]
    for h in range(8):
        put(None, dq_ref, h, nr64(slab(h), gdq) * 0.125)
    for h in range(4):
        put(rd_ref, rdb_ref, h, nr64(slab(8 + h), gdk))
        put(rd_ref, rdb_ref, 4 + h, slab(12 + h))
    for h in range(8):
        put(None, mq_ref, h, nr128(slab(16 + h), gmq) * SCALE128)
    for h in range(4):
        put(rm_ref, rmb_ref, h, nr128(slab(24 + h), gmk))
        put(rm_ref, rmb_ref, 4 + h, slab(28 + h))
    for h in range(16):
        put(None, nq_ref, h, nr128(slab(32 + h), gn_ref[0:1, :]) * SCALE128)
    for g in range(2):
        put(rn_ref, rnb_ref, g, nr128(slab(48 + g), gn_ref[1:2, :]))
        put(rn_ref, rnb_ref, 2 + g, slab(50 + g))
        put(rn_ref, rnb_ref, 4 + g, nr128(slab(52 + g), gn_ref[2:3, :]))
        put(rn_ref, rnb_ref, 6 + g, slab(54 + g))
        put(win_ref, winb_ref, g, nr128(slab(56 + g), gn_ref[3:4, :]))
        put(win_ref, winb_ref, 2 + g, slab(58 + g))
    if emit_chunks:
        xc_ref, slab_ref = rest
        n = tm // CMP_STRIDE
        for cg in range(4):
            slab_ref[...] = rn_ref[:, cg * LANES:(cg + 1) * LANES]
            for j in range(CMP_STRIDE):
                xc_ref[0, cg, :, j * LANES:(j + 1) * LANES] = slab_ref[pl.ds(j, n, stride=CMP_STRIDE), :]


def _rope_tables(pos):
    def tab(d):
        inv = ROPE_THETA ** (-jnp.arange(0, d, 2, dtype=F32) / d)
        ang = pos.astype(F32)[:, None] * inv[None, :]
        cos, sin = jnp.cos(ang), jnp.sin(ang)
        c = jnp.concatenate([cos, cos], axis=-1)
        s = jnp.concatenate([-sin, sin], axis=-1)
        rep = LANES // d
        return jnp.tile(c, (1, rep)), jnp.tile(s, (1, rep))
    c128, s128 = tab(128)
    c64, s64 = tab(64)
    return c128, s128, c64, s64


def post_project(proj, tables, gd, gm, gn, *, seq, batch):
    M = proj.shape[0]
    tm = min(M, 256)
    emit_chunks = seq >= tm
    nt = max(seq // tm, 1)
    row = lambda w: pl.BlockSpec((tm, w), lambda i: (i, 0))
    tab = pl.BlockSpec((tm, LANES), lambda i: (i % nt, 0))
    par = lambda r: pl.BlockSpec((r, LANES), lambda i: (0, 0))
    out_shapes = [
        jax.ShapeDtypeStruct((M, 1024), BF16),
        jax.ShapeDtypeStruct((M, 1024), F32), jax.ShapeDtypeStruct((M, 1024), BF16),
        jax.ShapeDtypeStruct((M, 1024), BF16),
        jax.ShapeDtypeStruct((M, 1024), F32), jax.ShapeDtypeStruct((M, 1024), BF16),
        jax.ShapeDtypeStruct((M, 2048), BF16),
        jax.ShapeDtypeStruct((M, 1024), F32), jax.ShapeDtypeStruct((M, 1024), BF16),
        jax.ShapeDtypeStruct((M, 512), F32), jax.ShapeDtypeStruct((M, 512), BF16),
    ]
    out_specs = [row(1024), row(1024), row(1024), row(1024), row(1024), row(1024), row(2048),
                 row(1024), row(1024), row(512), row(512)]
    if emit_chunks:
        nchunk = seq // CMP_STRIDE
        out_shapes.append(jax.ShapeDtypeStruct((batch, 4, nchunk, CMP_STRIDE * LANES), F32))
        out_specs.append(pl.BlockSpec((1, 4, tm // CMP_STRIDE, CMP_STRIDE * LANES),
                                      lambda i: (i // nt, 0, i % nt, 0)))
    return pl.pallas_call(
        functools.partial(_post_kernel, tm=tm, emit_chunks=emit_chunks), grid=(M // tm,),
        in_specs=[row(QKV_COLS), tab, tab, tab, tab, par(2), par(2), par(4)],
        out_specs=out_specs, out_shape=out_shapes,
        scratch_shapes=[pltpu.VMEM((tm, LANES), F32)] if emit_chunks else [],
        compiler_params=_cparams("parallel"), name="post_project",
    )(proj, *tables, gd, gm, gn)


def _online(carry, s, v, ok=None):
    m, l, acc = carry
    m_new = jnp.maximum(m, jnp.max(s, axis=-1, keepdims=True))
    alpha = jnp.exp(m - m_new)
    p = jnp.exp(s - m_new)
    if ok is not None:
        p = jnp.where(ok, p, 0.0)
    l = alpha * l + jnp.sum(p, axis=-1, keepdims=True)
    acc = alpha * acc + _dot(p.astype(BF16), v)
    return m_new, l, acc


def _init_carry(rows, width):
    return (jnp.full((rows, 1), NEG, F32), jnp.zeros((rows, 1), F32), jnp.zeros((rows, width), F32))


def _online_t(carry, s_t, v, ok=None):
    m, l, acc = carry
    m_new = jnp.maximum(m, jnp.max(s_t, axis=0, keepdims=True))
    alpha = jnp.exp(m - m_new)
    p = jnp.exp(s_t - m_new)
    if ok is not None:
        p = jnp.where(ok, p, 0.0)
    l = alpha * l + jnp.sum(p, axis=0, keepdims=True)
    acc = alpha * acc + _dot(v.astype(F32).T.astype(BF16), p.astype(BF16))
    return m_new, l, acc


def _init_carry_t(rows, width):
    return (jnp.full((1, rows), NEG, F32), jnp.zeros((1, rows), F32), jnp.zeros((width, rows), F32))


def _flash_reset(m_ref, l_ref, acc_ref):
    m_ref[...] = jnp.full(m_ref.shape, NEG, F32)
    l_ref[...] = jnp.zeros(l_ref.shape, F32)
    acc_ref[...] = jnp.zeros(acc_ref.shape, F32)


def _lambda(lam_ref, lam_init):
    lq = lam_ref[...]
    a = jnp.sum(lq[0:1, :] * lq[1:2, :], axis=-1, keepdims=True)
    b = jnp.sum(lq[2:3, :] * lq[3:4, :], axis=-1, keepdims=True)
    return jnp.exp(a) - jnp.exp(b) + lam_init


def _head_norm(o, g, post):
    ms = jnp.mean(o * o, axis=-1, keepdims=True)
    return o * lax.rsqrt(ms + EPS) * g * post


def _diff_prompt_kernel(lam_ref, gn_ref, q_ref, k_ref, v_ref, o_ref, *, tq, lam_init):
    i = pl.program_id(2)
    lane = lax.broadcasted_iota(I32, (tq, LANES), 1)
    q = q_ref[...]
    zero = jnp.zeros((tq, LANES), BF16)
    parts = []
    for r in range(2):
        qr = q[:, r * LANES:(r + 1) * LANES]
        parts.append(jnp.where(lane < 64, qr, zero))
        parts.append(jnp.where(lane < 64, zero, qr))
    Q = jnp.concatenate(parts, axis=0)
    R = 4 * tq

    def blk(kb):
        s0 = pl.multiple_of(kb * tq, tq)
        return k_ref[pl.ds(s0, tq), :], v_ref[pl.ds(s0, tq), :]

    def body(kb, carry):
        k, v = blk(kb)
        return _online_t(carry, _dot_nt(k, Q), v)

    carry = lax.fori_loop(0, i, body, _init_carry_t(R, LANES))
    k, v = blk(i)
    key = lax.broadcasted_iota(I32, (tq, R), 0)
    qrow = lax.broadcasted_iota(I32, (tq, R), 1) & (tq - 1)
    s = jnp.where(key <= qrow, _dot_nt(k, Q), NEG)
    m, l, acc = _online_t(carry, s, v)
    o = acc / jnp.maximum(l, TINY)
    lam = _lambda(lam_ref, lam_init)
    g = gn_ref[...]
    for r in range(2):
        o_r = o[:, (2 * r) * tq:(2 * r + 1) * tq] - lam * o[:, (2 * r + 1) * tq:(2 * r + 2) * tq]
        o_ref[:, r * LANES:(r + 1) * LANES] = _head_norm(o_r.T, g, 1.0 - lam_init).astype(o_ref.dtype)


def diff_prompt(dq, rows_b, lam_param, out_norm, lam_init, *, batch, seq):
    M = dq.shape[0]
    tq = 256
    nq = seq // tq
    return pl.pallas_call(
        functools.partial(_diff_prompt_kernel, tq=tq, lam_init=lam_init),
        grid=(batch, DIFF_KV_HEADS, nq),
        in_specs=[pl.BlockSpec((4, DIFF_DH), lambda b, g, i: (0, 0)),
                  pl.BlockSpec((1, LANES), lambda b, g, i: (0, 0)),
                  pl.BlockSpec((tq, 2 * LANES), lambda b, g, i: (b * nq + i, g)),
                  pl.BlockSpec((seq, LANES), lambda b, g, i: (b, g)),
                  pl.BlockSpec((seq, LANES), lambda b, g, i: (b, DIFF_KV_HEADS + g))],
        out_specs=pl.BlockSpec((tq, 2 * LANES), lambda b, g, i: (b * nq + i, g)),
        out_shape=jax.ShapeDtypeStruct((M, DIFF_HEADS * LANES), BF16),
        compiler_params=_cparams("parallel", "parallel", "parallel"), name="diff_prompt",
    )(lam_param, out_norm.reshape(1, LANES), dq, rows_b, rows_b)


def _rank_lt(vals, n_cand, lane):
    cnt = jnp.zeros(vals.shape, F32)
    for m in range(n_cand):
        col = vals[:, m:m + 1]
        beats = (col > vals) | ((col == vals) & (lane > m))
        cnt = cnt + jnp.where(beats, 1.0, 0.0)
    return cnt


def _rank_lt_t(vals, n_cand, idx):
    cnt = jnp.zeros(vals.shape, F32)
    for m in range(n_cand):
        row = vals[m:m + 1, :]
        beats = (row > vals) | ((row == vals) & (idx > m))
        cnt = cnt + jnp.where(beats, 1.0, 0.0)
    return cnt


def _moba_prompt_kernel(q_ref, kf_ref, k_ref, v_ref, o_ref, *, tq, nb):
    i = pl.program_id(2)
    q = q_ref[...]
    Q = jnp.concatenate([q[:, :LANES], q[:, LANES:]], axis=0)
    R = 2 * tq
    nbp = -(-nb // SUBLANES) * SUBLANES
    blkid = lax.broadcasted_iota(I32, (nbp, R), 0)
    kms = [jnp.mean(kf_ref[n * MOBA_BLOCK:(n + 1) * MOBA_BLOCK, :], axis=0, keepdims=True) for n in range(nb)]
    kmean = jnp.concatenate(kms + [jnp.zeros((LANES - nb, LANES), F32)], axis=0)
    gate = _dot_nt(kmean.astype(BF16), Q)[0:nbp]
    gate = jnp.where(blkid < i, gate, NEG)
    cnt = _rank_lt_t(gate, nb, blkid)
    sel = jnp.where((cnt < MOBA_TOPK) & (gate > NEG / 2), 1.0, 0.0)

    def blk(kb):
        s0 = pl.multiple_of(kb * tq, tq)
        return k_ref[pl.ds(s0, tq), :], v_ref[pl.ds(s0, tq), :]

    def body(kb, carry):
        k, v = blk(kb)
        on = jnp.sum(jnp.where(blkid == kb, sel, 0.0), axis=0, keepdims=True) > 0.5
        s = jnp.where(on, _dot_nt(k, Q), NEG)
        return _online_t(carry, s, v, ok=on)

    carry = lax.fori_loop(0, i, body, _init_carry_t(R, LANES))
    k, v = blk(i)
    key = lax.broadcasted_iota(I32, (tq, R), 0)
    qrow = lax.broadcasted_iota(I32, (tq, R), 1) & (tq - 1)
    s = jnp.where(key <= qrow, _dot_nt(k, Q), NEG)
    m, l, acc = _online_t(carry, s, v)
    o = acc / jnp.maximum(l, TINY)
    for r in range(2):
        o_ref[:, r * LANES:(r + 1) * LANES] = o[:, r * tq:(r + 1) * tq].T.astype(o_ref.dtype)


def moba_prompt(mq, rows_f, rows_b, *, batch, seq):
    M = mq.shape[0]
    tq = MOBA_BLOCK
    nq = seq // tq
    return pl.pallas_call(
        functools.partial(_moba_prompt_kernel, tq=tq, nb=seq // MOBA_BLOCK),
        grid=(batch, MOBA_KV_HEADS, nq),
        in_specs=[pl.BlockSpec((tq, 2 * LANES), lambda b, g, i: (b * nq + i, g)),
                  pl.BlockSpec((seq, LANES), lambda b, g, i: (b, g)),
                  pl.BlockSpec((seq, LANES), lambda b, g, i: (b, g)),
                  pl.BlockSpec((seq, LANES), lambda b, g, i: (b, MOBA_KV_HEADS + g))],
        out_specs=pl.BlockSpec((tq, 2 * LANES), lambda b, g, i: (b * nq + i, g)),
        out_shape=jax.ShapeDtypeStruct((M, MOBA_HEADS * LANES), BF16),
        compiler_params=_cparams("parallel", "parallel", "parallel"), name="moba_prompt",
    )(mq, rows_f, rows_b, rows_b)


def _compress_kernel(x_ref, pe_ref, w0_ref, w1_ref, w2_ref, o_ref, *, n):
    x = x_ref[0, 0]
    y0 = _dot((x + pe_ref[0, 0:1, :]).astype(BF16), w0_ref[0])
    y1 = _dot((x + pe_ref[0, 1:2, :]).astype(BF16), w1_ref[0])
    hid = y0 + pltpu.roll(y1, n - 1, 0)
    act = hid * jax.nn.sigmoid(hid)
    o_ref[0, 0] = _dot(act.astype(BF16), w2_ref[0]).astype(o_ref.dtype)


def compress(xc, pe, w0, w1, w2):
    B, _, n, W = xc.shape
    return pl.pallas_call(
        functools.partial(_compress_kernel, n=n), grid=(B, 4),
        in_specs=[pl.BlockSpec((1, 1, n, W), lambda b, c: (b, c, 0, 0)),
                  pl.BlockSpec((1, 2, W), lambda b, c: (c // 2, 0, 0)),
                  pl.BlockSpec((1, W, CMP_HIDDEN), lambda b, c: (c // 2, 0, 0)),
                  pl.BlockSpec((1, W, CMP_HIDDEN), lambda b, c: (c // 2, 0, 0)),
                  pl.BlockSpec((1, CMP_HIDDEN, LANES), lambda b, c: (c // 2, 0, 0))],
        out_specs=pl.BlockSpec((1, 1, n, LANES), lambda b, c: (b, c, 0, 0)),
        out_shape=jax.ShapeDtypeStruct((B, 4, n, LANES), BF16),
        compiler_params=_cparams("parallel", "parallel"), name="compress",
    )(xc, pe, w0, w1, w2)


def _nsa_prompt_kernel(q_ref, g_ref, ck_ref, cv_ref, sk_ref, sv_ref, wk_ref, wv_ref, imp_ref, exp_ref,
                       o_ref, msel_ref, *, tq, seq):
    i = pl.program_id(2)
    H = NSA_REP
    R = H * tq
    n_cmp = seq // CMP_STRIDE - 1
    n_sel = seq // SEL_BLOCK
    q = q_ref[...]
    Q = jnp.concatenate([q[:, r * LANES:(r + 1) * LANES] for r in range(H)], axis=0)
    row = lax.broadcasted_iota(I32, (tq, R), 0)
    qp = i * tq + (lax.broadcasted_iota(I32, (tq, R), 1) & (tq - 1))

    s = _dot_nt(ck_ref[0, 0], Q)
    ok = (row < n_cmp) & (row * CMP_STRIDE + (CMP_BLOCK - 1) <= qp)
    s = jnp.where(ok, s, NEG)
    p = jnp.where(ok, jnp.exp(s - jnp.max(s, axis=0, keepdims=True)), 0.0)
    pc = p / jnp.maximum(jnp.sum(p, axis=0, keepdims=True), TINY)
    o_c = _dot(cv_ref[0, 0].astype(F32).T.astype(BF16), pc.astype(BF16))

    pcs = pc[:, 0:tq]
    for r in range(1, H):
        pcs = pcs + pc[:, r * tq:(r + 1) * tq]
    imp = _dot_exact(imp_ref[...], pcs)[0:n_sel]
    blk_id = lax.broadcasted_iota(I32, (n_sel, tq), 0)
    cur = (i * tq + lax.broadcasted_iota(I32, (n_sel, tq), 1)) >> 6
    valid = blk_id <= cur
    forced = (blk_id == 0) | (blk_id == cur) | (blk_id == cur - 1)
    imp = jnp.where(valid, jnp.where(forced, FORCE_SCORE, imp), NEG)
    cnt = _rank_lt_t(imp, n_sel, blk_id)
    sel = jnp.where((cnt < SEL_TOPK) & (imp > NEG / 2), 1.0, 0.0)
    sel = jnp.concatenate([sel, jnp.zeros((LANES - n_sel, tq), F32)], axis=0)
    msel_ref[...] = _dot(exp_ref[...], sel.astype(BF16))

    tk = 2 * tq
    row_k = lax.broadcasted_iota(I32, (tk, tq), 0)
    qp_k = i * tq + lax.broadcasted_iota(I32, (tk, tq), 1)

    def blk(kref, vref, kb):
        s0 = pl.multiple_of(kb * tk, tk)
        return kref[pl.ds(s0, tk), :], vref[pl.ds(s0, tk), :]

    def attend(carry, k, v, ok_q):
        bias = jnp.concatenate([jnp.where(ok_q, 0.0, NEG)] * H, axis=1)
        return _online_t(carry, _dot_nt(k, Q) + bias, v)

    def sel_body(kb, carry):
        k, v = blk(sk_ref, sv_ref, kb)
        on = msel_ref[pl.ds(pl.multiple_of(kb * tk, tk), tk), :] > 0.5
        return attend(carry, k, v, on & (kb * tk + row_k <= qp_k))

    m, l, acc = lax.fori_loop(0, (i + 2) // 2, sel_body, _init_carry_t(R, LANES))
    o_s = acc / jnp.maximum(l, TINY)

    def win_body(kb, carry):
        k, v = blk(wk_ref, wv_ref, kb)
        dist = qp_k - (kb * tk + row_k)
        return attend(carry, k, v, (dist >= 0) & (dist <= WINDOW))

    first = jnp.maximum(i - WINDOW // tq, 0) // 2
    m, l, acc = lax.fori_loop(first, i // 2 + 1, win_body, _init_carry_t(R, LANES))
    o_w = acc / jnp.maximum(l, TINY)

    gates = g_ref[...].T
    for r in range(H):
        cs = slice(r * tq, (r + 1) * tq)
        o = (gates[r:r + 1] * o_c[:, cs] + gates[H + r:H + r + 1] * o_s[:, cs]
             + gates[2 * H + r:2 * H + r + 1] * o_w[:, cs])
        o_ref[:, r * LANES:(r + 1) * LANES] = o.T.astype(o_ref.dtype)


def _importance_matrix(n_tok_pad, n_tok, n_blk_pad, n_blk):
    r_ = SEL_BLOCK // CMP_STRIDE
    f_ = CMP_BLOCK // CMP_STRIDE - 1
    i = np.arange(n_tok_pad)[:, None]
    j = np.arange(n_blk_pad)[None, :]
    a = (i >= r_ * j - f_) & (i <= r_ * j + r_ - 1) & (i < n_tok) & (j < n_blk)
    return jnp.asarray(a.astype(np.float32))


def _expand_matrix(n_blk_pad, n_keys):
    j = np.arange(n_blk_pad)[:, None]
    t = np.arange(n_keys)[None, :]
    return jnp.asarray((t // SEL_BLOCK == j).astype(np.float32), dtype=BF16)


def nsa_prompt(nq, ng, cmp_tok, rows_b, win_b, *, batch, seq):
    M = nq.shape[0]
    tq = 128
    nt = seq // tq
    G = NSA_KV_HEADS
    n_tok = seq // CMP_STRIDE
    assert n_tok == LANES and tq == LANES and (seq // SEL_BLOCK) % SUBLANES == 0
    imp_m = _importance_matrix(LANES, n_tok - 1, LANES, seq // SEL_BLOCK).T
    exp_m = _expand_matrix(LANES, seq).T
    kv = lambda c: pl.BlockSpec((seq, LANES), lambda b, g, i: (b, c + g))
    return pl.pallas_call(
        functools.partial(_nsa_prompt_kernel, tq=tq, seq=seq), grid=(batch, G, nt),
        in_specs=[pl.BlockSpec((tq, NSA_REP * LANES), lambda b, g, i: (b * nt + i, g)),
                  pl.BlockSpec((tq, LANES), lambda b, g, i: (b * nt + i, g)),
                  pl.BlockSpec((1, 1, n_tok, LANES), lambda b, g, i: (b, g, 0, 0)),
                  pl.BlockSpec((1, 1, n_tok, LANES), lambda b, g, i: (b, G + g, 0, 0)),
                  kv(4), kv(6), kv(0), kv(2),
                  pl.BlockSpec((LANES, LANES), lambda b, g, i: (0, 0)),
                  pl.BlockSpec((seq, LANES), lambda b, g, i: (0, 0))],
        out_specs=pl.BlockSpec((tq, NSA_REP * LANES), lambda b, g, i: (b * nt + i, g)),
        out_shape=jax.ShapeDtypeStruct((M, NSA_HEADS * LANES), BF16),
        scratch_shapes=[pltpu.VMEM((seq, tq), F32)],
        compiler_params=_cparams("parallel", "parallel", "arbitrary"), name="nsa_prompt",
    )(nq, ng, cmp_tok, cmp_tok, rows_b, rows_b, win_b, win_b, imp_m, exp_m)


def _slot_scores(q, raw_bf16, key_slot, nslots):
    s = _dot_nt(q, raw_bf16)
    col = lax.broadcasted_iota(I32, s.shape, 1) & (nslots - 1)
    return s, col == key_slot


def _slot_update(carry, s, ok, raw_bf16, shift):
    m_old, l, acc = carry
    m_new = jnp.maximum(m_old, jnp.max(s, axis=-1, keepdims=True))
    alpha = jnp.exp(m_old - m_new)
    p = jnp.where(ok, jnp.exp(s - m_new), 0.0)
    l = alpha * l + jnp.sum(p, axis=-1, keepdims=True)
    acc = alpha * acc + _dot(pltpu.roll(p, shift, 1).astype(BF16), raw_bf16)
    return m_new, l, acc


def _with_new(m, l, acc, s_new, vn, on=True):
    s_new = jnp.where(on, s_new, NEG)
    m2 = jnp.maximum(m, s_new)
    a2 = jnp.exp(m - m2)
    pn = jnp.where(on, jnp.exp(s_new - m2), 0.0)
    return (a2 * acc + pn * vn) / jnp.maximum(a2 * l + pn, TINY)


def _diff_sample_kernel(pt_ref, lam_ref, gn_ref, q_ref, kn_ref, vn_ref, *refs, P, lam_init):
    pages = refs[:P]
    o_ref = refs[P]
    m_ref, l_ref, acc_ref = refs[P + 1:]
    j = pl.program_id(1)

    @pl.when(j == 0)
    def _():
        _flash_reset(m_ref, l_ref, acc_ref)

    Q = q_ref[0]
    raw = jnp.concatenate([pages[p][0, 0].astype(BF16) for p in range(P)], axis=0)
    s = _dot_nt(Q, raw)
    slot = lax.broadcasted_iota(I32, s.shape, 1) & (SLOTS - 1)
    grp = lax.broadcasted_iota(I32, s.shape, 0) >> 2
    ok = slot == grp
    m_ref[...], l_ref[...], acc_ref[...] = _slot_update(
        (m_ref[...], l_ref[...], acc_ref[...]), jnp.where(ok, s, NEG), ok, raw, DIFF_KV_HEADS)

    @pl.when(j == pl.num_programs(1) - 1)
    def _():
        s_new = jnp.sum(Q.astype(F32) * kn_ref[0].astype(F32), axis=-1, keepdims=True)
        a = _with_new(m_ref[...], l_ref[...], acc_ref[...], s_new, vn_ref[0].astype(F32))
        lam = _lambda(lam_ref, lam_init)
        g = gn_ref[...]
        outs = []
        for h in range(DIFF_HEADS):
            o = a[2 * h:2 * h + 1] - lam * a[2 * h + 1:2 * h + 2]
            outs.append(_head_norm(o, g, 1.0 - lam_init))
        o_ref[0] = jnp.concatenate(outs, axis=0).astype(o_ref.dtype)


def diff_sample(q16, kn16, vn16, cache, page_table_flat, lam_param, out_norm, lam_init, *, layer, batch, n_pages):
    P = 8
    R = PAGE_SIZE * SLOTS
    page = lambda p: pl.BlockSpec((1, 1, R, LANES), lambda b, j, pt: (layer, pt[b * n_pages + j * P + p], 0, 0))
    per_b = pl.BlockSpec((1, 16, LANES), lambda b, j, pt: (b, 0, 0))
    gs = pltpu.PrefetchScalarGridSpec(
        num_scalar_prefetch=1, grid=(batch, n_pages // P),
        in_specs=[pl.BlockSpec((4, DIFF_DH), lambda b, j, pt: (0, 0)),
                  pl.BlockSpec((1, LANES), lambda b, j, pt: (0, 0)),
                  per_b, per_b, per_b] + [page(p) for p in range(P)],
        out_specs=pl.BlockSpec((1, DIFF_HEADS, LANES), lambda b, j, pt: (b, 0, 0)),
        scratch_shapes=[pltpu.VMEM((16, 1), F32), pltpu.VMEM((16, 1), F32), pltpu.VMEM((16, LANES), F32)])
    return pl.pallas_call(
        functools.partial(_diff_sample_kernel, P=P, lam_init=lam_init), grid_spec=gs,
        out_shape=jax.ShapeDtypeStruct((batch, DIFF_HEADS, LANES), BF16),
        compiler_params=_cparams("parallel", "arbitrary"), name="diff_sample",
    )(page_table_flat, lam_param, out_norm.reshape(1, LANES), q16, kn16, vn16, *([cache] * P))


def _moba_gate_kernel(pt_ref, q_ref, *refs, nblk, bps):
    pages = refs[:2 * bps]
    o_ref, gate_ref = refs[2 * bps:]
    n = pl.program_id(1)
    lane = lax.broadcasted_iota(I32, (MOBA_HEADS, LANES), 1)

    @pl.when(n == 0)
    def _():
        gate_ref[...] = jnp.full(gate_ref.shape, NEG, F32)

    q = q_ref[0].astype(F32)
    gate = gate_ref[...]
    for t in range(bps):
        ksum = (jnp.sum(pages[2 * t][0, 0].reshape(PAGE_SIZE, SLOTS, LANES), axis=0)
                + jnp.sum(pages[2 * t + 1][0, 0].reshape(PAGE_SIZE, SLOTS, LANES), axis=0))
        kmean = (ksum * (1.0 / MOBA_BLOCK)).astype(BF16).astype(F32)
        per_head = jnp.concatenate([kmean[h // 2:h // 2 + 1] for h in range(MOBA_HEADS)], axis=0)
        g = jnp.sum(q * per_head, axis=-1, keepdims=True)
        gate = jnp.where(lane == n * bps + t, g, gate)
    gate_ref[...] = gate

    @pl.when(n == nblk // bps - 1)
    def _():
        cnt = _rank_lt(gate, nblk, lane)
        lanef = lane.astype(F32)
        out = jnp.zeros(gate.shape, F32)
        for k in range(MOBA_TOPK):
            hit = cnt == float(k)
            idx = jnp.sum(jnp.where(hit, lanef, 0.0), axis=-1, keepdims=True)
            val = jnp.max(jnp.where(hit, gate, NEG), axis=-1, keepdims=True)
            out = out + jnp.where(lane == k, idx, 0.0) + jnp.where((lane == 4 + k) & (val > NEG / 2), 1.0, 0.0)
        o_ref[0] = out.astype(I32)


def moba_gate_sample(q, cache, page_table_flat, *, layer, batch, n_pages):
    nblk = n_pages * PAGE_SIZE // MOBA_BLOCK
    ppb = MOBA_BLOCK // PAGE_SIZE
    bps = 4 if nblk % 4 == 0 else 1
    assert ppb == 2 and nblk <= LANES
    R = PAGE_SIZE * SLOTS
    page = lambda p: pl.BlockSpec((1, 1, R, LANES),
                                  lambda b, n, pt: (layer, pt[b * n_pages + n * bps * ppb + p], 0, 0))
    gs = pltpu.PrefetchScalarGridSpec(
        num_scalar_prefetch=1, grid=(batch, nblk // bps),
        in_specs=[pl.BlockSpec((1, MOBA_HEADS, LANES), lambda b, n, pt: (b, 0, 0))]
        + [page(p) for p in range(bps * ppb)],
        out_specs=pl.BlockSpec((1, MOBA_HEADS, LANES), lambda b, n, pt: (b, 0, 0)),
        scratch_shapes=[pltpu.VMEM((MOBA_HEADS, LANES), F32)])
    return pl.pallas_call(
        functools.partial(_moba_gate_kernel, nblk=nblk, bps=bps), grid_spec=gs,
        out_shape=jax.ShapeDtypeStruct((batch, MOBA_HEADS, LANES), I32),
        compiler_params=_cparams("parallel", "arbitrary"), name="moba_gate_sample",
    )(page_table_flat, q, *([cache] * (bps * ppb)))


def _moba_sample_kernel(pg_ref, ok_ref, q_ref, kn_ref, vn_ref, *refs, npage):
    pages, o_ref = refs[:npage], refs[npage]
    b, h = pl.program_id(0), pl.program_id(1)
    Q = q_ref[0, 0]
    carry = _init_carry(8, LANES)
    for t in range(npage):
        raw = pages[t][0, 0].astype(BF16)
        on = ok_ref[(b * MOBA_HEADS + h) * MOBA_TOPK + t // (npage // MOBA_TOPK)] > 0
        s, is_key = _slot_scores(Q, raw, h // 2, SLOTS)
        ok = is_key & on
        carry = _slot_update(carry, jnp.where(ok, s, NEG), ok, raw, MOBA_KV_HEADS)
    s_new = jnp.sum(Q.astype(F32) * kn_ref[0].astype(F32), axis=-1, keepdims=True)
    o = _with_new(*carry, s_new, vn_ref[0].astype(F32))
    o_ref[0] = o[0:1].astype(o_ref.dtype)


def moba_sample(q8, kv_new, cache, sel_pages, sel_ok, *, layer, batch):
    npage = MOBA_TOPK * (MOBA_BLOCK // PAGE_SIZE)
    G = MOBA_KV_HEADS
    R = PAGE_SIZE * SLOTS
    page = lambda t: pl.BlockSpec(
        (1, 1, R, LANES), lambda b, h, pgs, oks: (layer, pgs[(b * MOBA_HEADS + h) * npage + t], 0, 0))
    gs = pltpu.PrefetchScalarGridSpec(
        num_scalar_prefetch=2, grid=(batch, MOBA_HEADS),
        in_specs=[pl.BlockSpec((1, 1, 8, LANES), lambda b, h, pgs, oks: (b, h, 0, 0)),
                  pl.BlockSpec((1, 1, LANES), lambda b, h, pgs, oks: (b, 0, h // 2)),
                  pl.BlockSpec((1, 1, LANES), lambda b, h, pgs, oks: (b, 0, G + h // 2))]
        + [page(t) for t in range(npage)],
        out_specs=pl.BlockSpec((1, 1, LANES), lambda b, h, pgs, oks: (b, 0, h)))
    return pl.pallas_call(
        functools.partial(_moba_sample_kernel, npage=npage), grid_spec=gs,
        out_shape=jax.ShapeDtypeStruct((batch, 1, MOBA_HEADS * LANES), BF16),
        compiler_params=_cparams("parallel", "parallel"), name="moba_sample",
    )(sel_pages, sel_ok, q8, kv_new, kv_new, *([cache] * npage))


def _nsa_chunk_kernel(pt_ref, *refs, P):
    pages, o_ref = refs[:P], refs[P]
    n = PAGE_SIZE // CMP_STRIDE
    for p in range(P):
        for cg in range(4):
            for j in range(CMP_STRIDE):
                o_ref[0, cg, p * n:(p + 1) * n, j * LANES:(j + 1) * LANES] = (
                    pages[p][0, 0, pl.ds(j * SLOTS + cg, n, stride=CMP_STRIDE * SLOTS), :])


def nsa_chunk_sample(cache, page_table_flat, *, layer, batch, n_pages):
    n = PAGE_SIZE // CMP_STRIDE
    P = 4
    R = PAGE_SIZE * SLOTS
    page = lambda p: pl.BlockSpec((1, 1, R, LANES), lambda b, s, pt: (layer, pt[b * n_pages + s * P + p], 0, 0))
    gs = pltpu.PrefetchScalarGridSpec(
        num_scalar_prefetch=1, grid=(batch, n_pages // P),
        in_specs=[page(p) for p in range(P)],
        out_specs=pl.BlockSpec((1, 4, P * n, CMP_STRIDE * LANES), lambda b, s, pt: (b, 0, s, 0)))
    return pl.pallas_call(
        functools.partial(_nsa_chunk_kernel, P=P), grid_spec=gs,
        out_shape=jax.ShapeDtypeStruct((batch, 4, n_pages * n, CMP_STRIDE * LANES), F32),
        compiler_params=_cparams("parallel", "parallel"), name="nsa_chunk_sample",
    )(page_table_flat, *([cache] * P))


def _nsa_cmp_sample_kernel(q_ref, ck_ref, cv_ref, imp_ref, oc_ref, sel_ref, *, n_tok, n_valid, cur, n_blk_pad):
    Q = q_ref[0, 0]
    lane = lax.broadcasted_iota(I32, (NSA_REP, n_tok), 1)
    ok = lane < n_valid
    s = jnp.where(ok, _dot_nt(Q, ck_ref[0, 0]), NEG)
    p = jnp.where(ok, jnp.exp(s - jnp.max(s, axis=-1, keepdims=True)), 0.0)
    pc = p / jnp.maximum(jnp.sum(p, axis=-1, keepdims=True), TINY)
    oc_ref[0, 0] = _dot(pc.astype(BF16), cv_ref[0, 0])
    pcs = jnp.broadcast_to(jnp.sum(pc, axis=0, keepdims=True), (8, n_tok))
    imp = _dot_exact(pcs, imp_ref[...])[0:1]
    blk = lax.broadcasted_iota(I32, (1, n_blk_pad), 1)
    valid = blk <= cur
    forced = (blk == 0) | (blk == cur) | (blk == cur - 1)
    imp = jnp.where(valid, jnp.where(forced, FORCE_SCORE, imp), NEG)
    rows = jnp.broadcast_to(imp, (n_blk_pad, n_blk_pad))
    cols = rows.T
    mi = lax.broadcasted_iota(I32, (n_blk_pad, n_blk_pad), 0)
    ji = lax.broadcasted_iota(I32, (n_blk_pad, n_blk_pad), 1)
    beats = (cols > rows) | ((cols == rows) & (mi < ji))
    cnt = jnp.sum(jnp.where(beats, 1.0, 0.0), axis=0, keepdims=True)
    blkf = blk.astype(F32)
    lane_o = lax.broadcasted_iota(I32, (1, LANES), 1)
    out = jnp.zeros((1, LANES), F32)
    for k in range(SEL_TOPK):
        hit = cnt == float(k)
        idx = jnp.sum(jnp.where(hit, blkf, 0.0), axis=-1, keepdims=True)
        val = jnp.max(jnp.where(hit, imp, NEG), axis=-1, keepdims=True)
        out = out + jnp.where(lane_o == k, idx, 0.0) + jnp.where((lane_o == SEL_TOPK + k) & (val > NEG / 2), 1.0, 0.0)
    sel_ref[0, 0] = jnp.broadcast_to(out, (8, LANES)).astype(I32)


def nsa_cmp_sample(q8, cmp_tok, *, batch, past_len):
    G = NSA_KV_HEADS
    n_tok = cmp_tok.shape[2]
    n_valid = n_tok - 1
    cur = past_len // SEL_BLOCK
    n_blk = cur + 1
    n_blk_pad = -(-n_blk // LANES) * LANES
    imp_m = _importance_matrix(n_tok, n_tok, n_blk_pad, n_blk)
    return pl.pallas_call(
        functools.partial(_nsa_cmp_sample_kernel, n_tok=n_tok, n_valid=n_valid, cur=cur, n_blk_pad=n_blk_pad),
        grid=(batch, G),
        in_specs=[pl.BlockSpec((1, 1, NSA_REP, LANES), lambda b, g: (b, g, 0, 0)),
                  pl.BlockSpec((1, 1, n_tok, LANES), lambda b, g: (b, g, 0, 0)),
                  pl.BlockSpec((1, 1, n_tok, LANES), lambda b, g: (b, G + g, 0, 0)),
                  pl.BlockSpec((n_tok, n_blk_pad), lambda b, g: (0, 0))],
        out_specs=[pl.BlockSpec((1, 1, NSA_REP, LANES), lambda b, g: (b, g, 0, 0)),
                   pl.BlockSpec((1, 1, 8, LANES), lambda b, g: (b, g, 0, 0))],
        out_shape=[jax.ShapeDtypeStruct((batch, G, NSA_REP, LANES), F32),
                   jax.ShapeDtypeStruct((batch, G, 8, LANES), I32)],
        compiler_params=_cparams("parallel", "parallel"), name="nsa_cmp_sample",
    )(q8, cmp_tok, cmp_tok, imp_m)


def _nsa_sel_sample_kernel(pg_ref, hb_ref, ok_ref, own_ref, q_ref, gt_ref, oc_ref, kn_ref, vn_ref, wkn_ref,
                           wvn_ref, win_ref, *refs):
    blks, o_ref = refs[:SEL_TOPK], refs[SEL_TOPK]
    b, g = pl.program_id(0), pl.program_id(1)
    G = NSA_KV_HEADS
    Q = q_ref[0, 0]
    carry = _init_carry(NSA_REP, LANES)
    for t in range(SEL_TOPK):
        raw = blks[t][0, 0].astype(BF16)
        on = ok_ref[(b * G + g) * SEL_TOPK + t] > 0
        s, is_key = _slot_scores(Q, raw, 2 * G + g, SLOTS)
        ok = is_key & on
        carry = _slot_update(carry, jnp.where(ok, s, NEG), ok, raw, G)
    Qf = Q.astype(F32)

    def new_score(kn_ref):
        return jnp.sum(Qf * kn_ref[0].astype(F32), axis=-1, keepdims=True)

    own = own_ref[b * G + g] > 0
    o_s = _with_new(*carry, new_score(kn_ref), vn_ref[0].astype(F32), own)
    raww = win_ref[0, 0].astype(BF16)
    sw, okw = _slot_scores(Q, raww, g, 2 * G)
    cw = _slot_update(_init_carry(NSA_REP, LANES), jnp.where(okw, sw, NEG), okw, raww, G)
    o_w = _with_new(*cw, new_score(wkn_ref), wvn_ref[0].astype(F32))
    gt = gt_ref[0, 0]
    o_ref[0, 0] = gt[:, 0:1] * oc_ref[0, 0] + gt[:, 1:2] * o_s + gt[:, 2:3] * o_w


def nsa_sel_sample(q8, gates8, o_c, rows_new, win_new, state_win, cache, sel_pages, sel_half, sel_ok, sel_own, *,
                   layer, batch):
    G = NSA_KV_HEADS
    wrows = state_win.shape[2]
    idx = lambda b, g, t: (b * G + g) * SEL_TOPK + t
    new = lambda c: pl.BlockSpec((1, 1, LANES), lambda b, g, pgs, hbs, oks, own: (b, 0, c + g))
    per_bg = pl.BlockSpec((1, 1, NSA_REP, LANES), lambda b, g, pgs, hbs, oks, own: (b, g, 0, 0))
    blk = lambda t: pl.BlockSpec(
        (1, 1, SEL_BLOCK * SLOTS, LANES),
        lambda b, g, pgs, hbs, oks, own: (layer, pgs[idx(b, g, t)], hbs[idx(b, g, t)], 0))
    gs = pltpu.PrefetchScalarGridSpec(
        num_scalar_prefetch=4, grid=(batch, G),
        in_specs=[per_bg, per_bg, per_bg, new(4), new(6), new(0), new(2),
                  pl.BlockSpec((1, 1, wrows, LANES), lambda b, g, pgs, hbs, oks, own: (layer, b, 0, 0))]
        + [blk(t) for t in range(SEL_TOPK)],
        out_specs=per_bg)
    return pl.pallas_call(
        _nsa_sel_sample_kernel, grid_spec=gs,
        out_shape=jax.ShapeDtypeStruct((batch, G, NSA_REP, LANES), F32),
        compiler_params=_cparams("parallel", "parallel"), name="nsa_sel_sample",
    )(sel_pages, sel_half, sel_ok, sel_own, q8, gates8, o_c, rows_new, rows_new, win_new, win_new,
      state_win, *([cache] * SEL_TOPK))


def _prep_weights(w_in, ffn1_w_down, ffn2_w_down, w_branch, w_out, nsa_cmp_w1, nsa_cmp_w2, nsa_cmp_pos):
    depth = w_in.shape[0]
    bf = lambda w: w.astype(BF16)
    src = np.zeros((NSA_KV_HEADS, 3, NSA_REP), np.int32)
    for g in range(NSA_KV_HEADS):
        for t in range(3):
            for r in range(NSA_REP):
                src[g, t, r] = (g * NSA_REP + r) * 3 + t
    w_ng = w_in[:, :, QKV_COLS:QKV_COLS + NG_COLS][:, :, src.reshape(NSA_KV_HEADS, 3 * NSA_REP)]
    w_ng = jnp.pad(w_ng, ((0, 0), (0, 0), (0, 0), (0, LANES - 3 * NSA_REP))).reshape(depth, D_MODEL, NG_PAD)
    W = CMP_STRIDE * HEAD_DIM
    return dict(
        down1=bf(ffn1_w_down), down2=bf(ffn2_w_down), qkv=bf(w_in[:, :, :QKV_COLS]), ng=bf(w_ng),
        mg=bf(w_in[:, :, QKV_COLS + NG_COLS:]),
        branch=bf(w_branch), out=bf(w_out),
        cw0=bf(nsa_cmp_w1[:, :, :CMP_STRIDE].reshape(depth, 2, W, CMP_HIDDEN)),
        cw1=bf(nsa_cmp_w1[:, :, CMP_STRIDE:].reshape(depth, 2, W, CMP_HIDDEN)),
        cw2=bf(nsa_cmp_w2),
        cpe=nsa_cmp_pos.reshape(depth, 2, 2, W),
    )


def _ffn(x, norm_g, w_gu, w_down, layer):
    h = rmsnorm_bf16(x, norm_g)
    act = matmul_swiglu(h, w_gu, layer)
    return matmul_residual(act, w_down, layer, x, 0.5, tm=512, tn=512, name="ffn_down")


def _project(x, mix_g, wp, layer, tables, gd, gm, gn, *, seq, batch):
    h = rmsnorm_bf16(x, mix_g)
    proj = matmul(h, wp['qkv'], layer, n_cols=QKV_COLS, tn=512, name="in_proj_qkv")
    ng = matmul(h, wp['ng'], layer, n_cols=NG_PAD, tn=NG_PAD, sigmoid=True, name="in_proj_gate")
    mg = matmul(h, wp['mg'], layer, n_cols=MG_COLS, tn=512, sigmoid=True, name="in_proj_merge_gate")
    post = post_project(proj, tables, gd, gm, gn, seq=seq, batch=batch)
    return post, ng, mg


def _merge(x, o_d, o_m, o_n, mg, wp, layer):
    u = matmul_merge(o_d, o_m, o_n, wp['branch'], layer, mg)
    return matmul_residual(u, wp['out'], layer, x, 1.0, tm=1024, tn=512, name="out_proj")


def kernel(x_prompt, x_sample, cache_diff, cache_moba, cache_nsa, state_nsa_win, page_table, ffn1_norm, ffn1_w_gu, ffn1_w_down, mix_norm, w_in, diff_qk_norm, diff_lambda, diff_out_norm, moba_qk_norm, nsa_qk_norm, nsa_cmp_w1, nsa_cmp_w2, nsa_cmp_pos, w_branch, w_out, ffn2_norm, ffn2_w_gu, ffn2_w_down):
    B, T, D = x_prompt.shape
    Bs = x_sample.shape[0]
    depth = ffn1_norm.shape[0]
    n_pool = cache_diff.shape[1]
    n_pages = page_table.shape[1]
    past_len = n_pages * PAGE_SIZE
    MS = 16
    assert x_sample.shape[1] == 1 and Bs <= MS and T % MOBA_BLOCK == 0

    xp = x_prompt.reshape(B * T, D)
    xs = jnp.pad(x_sample.reshape(Bs, D), ((0, MS - Bs), (0, 0)))
    c_diff = cache_diff.reshape(depth, n_pool, PAGE_SIZE * SLOTS, HEAD_DIM)
    c_moba = cache_moba.reshape(depth, n_pool, PAGE_SIZE * SLOTS, HEAD_DIM)
    c_nsa = cache_nsa.reshape(depth, n_pool, PAGE_SIZE * SLOTS, HEAD_DIM)
    wlen = state_nsa_win.shape[2]
    s_win = state_nsa_win.reshape(depth, Bs, wlen * 2 * NSA_KV_HEADS, HEAD_DIM)
    pt_flat = page_table.reshape(-1).astype(I32)
    tab_p = _rope_tables(jnp.arange(T, dtype=I32))
    tab_s = _rope_tables(jnp.full((MS,), past_len, I32))
    wp = _prep_weights(w_in, ffn1_w_down, ffn2_w_down, w_branch, w_out, nsa_cmp_w1, nsa_cmp_w2, nsa_cmp_pos)

    outs = [[] for _ in range(8)]
    for l in range(depth):
        lam_init = 0.8 - 0.6 * math.exp(-0.3 * l)
        gd = jnp.tile(diff_qk_norm[l], (1, 2))
        gm, gn = moba_qk_norm[l], nsa_qk_norm[l]
        lam_p = diff_lambda[l]
        cmp_w = (wp['cpe'][l], wp['cw0'][l], wp['cw1'][l], wp['cw2'][l])

        xp = _ffn(xp, ffn1_norm[l], ffn1_w_gu, wp['down1'], l)
        xs = _ffn(xs, ffn1_norm[l], ffn1_w_gu, wp['down1'], l)

        post, ng, mg = _project(xp, mix_norm[l], wp, l, tab_p, gd, gm, gn, seq=T, batch=B)
        dq, rd, rdb, mq, rm, rmb, nq, rn, rnb, win, winb, xc = post
        o_d = diff_prompt(dq, rdb, lam_p, diff_out_norm[l], lam_init, batch=B, seq=T)
        o_m = moba_prompt(mq, rm, rmb, batch=B, seq=T)
        cmp_tok = compress(xc, *cmp_w)
        o_n = nsa_prompt(nq, ng, cmp_tok, rnb, winb, batch=B, seq=T)
        xp = _merge(xp, o_d, o_m, o_n, mg, wp, l)
        outs[0].append(rd.reshape(B, T, 2, DIFF_KV_HEADS, HEAD_DIM))
        outs[2].append(rm.reshape(B, T, 2, MOBA_KV_HEADS, HEAD_DIM))
        outs[4].append(rn.reshape(B, T, 4, NSA_KV_HEADS, HEAD_DIM))
        wkeep = min(WINDOW, T)
        outs[6].append(win.reshape(B, T, 2, NSA_KV_HEADS, HEAD_DIM)[:, T - wkeep:])

        post, ng, mg = _project(xs, mix_norm[l], wp, l, tab_s, gd, gm, gn, seq=1, batch=Bs)
        dq, rd, rdb, mq, rm, rmb, nq, rn, rnb, win, winb = post
        q5 = dq[:Bs].reshape(Bs, DIFF_KV_HEADS, 2, 2, DIFF_DH)
        eye_c = jnp.eye(2, dtype=BF16)
        q16 = (q5[:, :, :, :, None, :] * eye_c[None, None, None, :, :, None]).reshape(Bs, 16, LANES)
        kv4 = rdb[:Bs].reshape(Bs, 2, DIFF_KV_HEADS, 1, LANES)
        kvn16 = jnp.broadcast_to(kv4, (Bs, 2, DIFF_KV_HEADS, 4, LANES)).reshape(Bs, 2, 16, LANES)
        o_d = diff_sample(q16, kvn16[:, 0], kvn16[:, 1], c_diff, pt_flat, lam_p, diff_out_norm[l], lam_init,
                          layer=l, batch=Bs, n_pages=n_pages)
        msel = moba_gate_sample(mq[:Bs].reshape(Bs, MOBA_HEADS, LANES), c_moba, pt_flat,
                                layer=l, batch=Bs, n_pages=n_pages)
        m_idx = msel[:, :, 0:MOBA_TOPK]
        m_ok = msel[:, :, 4:4 + MOBA_TOPK]
        ppb = MOBA_BLOCK // PAGE_SIZE
        logical = (m_idx[..., None] * ppb + jnp.arange(ppb, dtype=I32)).reshape(Bs, MOBA_HEADS * MOBA_TOPK * ppb)
        m_pages = jnp.take_along_axis(page_table.astype(I32), logical, axis=1)
        q8 = jnp.broadcast_to(mq[:Bs].reshape(Bs, MOBA_HEADS, 1, HEAD_DIM), (Bs, MOBA_HEADS, 8, HEAD_DIM))
        o_m = moba_sample(q8, rmb[:Bs].reshape(Bs, 1, 1024), c_moba, m_pages.reshape(-1), m_ok.reshape(-1),
                          layer=l, batch=Bs)
        xcs = nsa_chunk_sample(c_nsa, pt_flat, layer=l, batch=Bs, n_pages=n_pages)
        cmp_tok = compress(xcs, *cmp_w)
        nq8 = nq[:Bs].reshape(Bs, NSA_KV_HEADS, NSA_REP, HEAD_DIM)
        o_c, nsel = nsa_cmp_sample(nq8, cmp_tok, batch=Bs, past_len=past_len)
        n_idx = nsel[:, :, 0, 0:SEL_TOPK]
        n_ok = nsel[:, :, 0, SEL_TOPK:2 * SEL_TOPK]
        bpp = PAGE_SIZE // SEL_BLOCK
        in_past = n_idx < past_len // SEL_BLOCK
        n_log = jnp.where(in_past, n_idx // bpp, 0)
        n_pages_sel = jnp.take_along_axis(page_table.astype(I32), n_log.reshape(Bs, -1), axis=1)
        n_half = jnp.where(in_past, n_idx % bpp, 0)
        n_on = (n_ok > 0) & in_past
        n_own = jnp.any((n_ok > 0) & (n_idx == past_len // SEL_BLOCK), axis=-1)
        gates8 = jnp.pad(ng[:Bs].reshape(Bs, NSA_KV_HEADS, LANES)[:, :, :3 * NSA_REP]
                         .reshape(Bs, NSA_KV_HEADS, 3, NSA_REP).transpose(0, 1, 3, 2),
                         ((0, 0), (0, 0), (0, 0), (0, LANES - 3)))
        o_n = nsa_sel_sample(nq8, gates8, o_c, rnb[:Bs].reshape(Bs, 1, 1024), winb[:Bs].reshape(Bs, 1, 512),
                             s_win, c_nsa, n_pages_sel.reshape(-1), n_half.reshape(-1).astype(I32),
                             n_on.reshape(-1).astype(I32), n_own.reshape(-1).astype(I32), layer=l, batch=Bs)
        pad_rows = lambda a, w: jnp.pad(a.reshape(Bs, w).astype(BF16), ((0, MS - Bs), (0, 0)))
        xs = _merge(xs, pad_rows(o_d, 1024), pad_rows(o_m, 1024), pad_rows(o_n, 2048), mg, wp, l)
        outs[1].append(rd[:Bs].reshape(Bs, 1, 2, DIFF_KV_HEADS, HEAD_DIM))
        outs[3].append(rm[:Bs].reshape(Bs, 1, 2, MOBA_KV_HEADS, HEAD_DIM))
        outs[5].append(rn[:Bs].reshape(Bs, 1, 4, NSA_KV_HEADS, HEAD_DIM))
        new_win = win[:Bs].reshape(Bs, 1, 2, NSA_KV_HEADS, HEAD_DIM)
        outs[7].append(jnp.concatenate([state_nsa_win[l], new_win], axis=1)[:, 1:])

        xp = _ffn(xp, ffn2_norm[l], ffn2_w_gu, wp['down2'], l)
        xs = _ffn(xs, ffn2_norm[l], ffn2_w_gu, wp['down2'], l)

    return (xp.reshape(B, T, D), xs[:Bs].reshape(Bs, 1, D)) + tuple(jnp.stack(o) for o in outs)
```
